```python
import jax, jax.numpy as jnp
from jax import lax
import numpy as np

D_MODEL = 2048
BATCH = 2
SEQ = 16384
DEPTH = 2

A_HEAD_DIM = 128
A_WIDTH = D_MODEL // 2
A_HEADS = A_WIDTH // A_HEAD_DIM
CHUNK = 64
F_MIN = 1e-6
NA_HEAD_DIM = 128
NA_WIDTH = D_MODEL // 2
NA_HEADS = NA_WIDTH // NA_HEAD_DIM
GRID_W = 64
WIN_R_MAX = 8
WIN_C = 16
FFN_HIDDEN = -(-(8 * D_MODEL) // (3 * 256)) * 256
PLE_DIM = 256
DEEPNORM_ALPHA = (2 * DEPTH) ** 0.25
DEEPNORM_BETA = (8 * DEPTH) ** -0.25
LN_EPS = 1e-5
RMS_EPS = 1e-6
SPLIT_SIZES = (A_WIDTH,) * 5 + (NA_WIDTH,) * 3 + (D_MODEL, D_MODEL)
N_IN = sum(SPLIT_SIZES)

kernel_name = "hgrn2_natten2d_gated_hybrid_deepnorm"


def layer_norm(x, g, b):
    xf = x.astype(jnp.float32)
    mu = jnp.mean(xf, axis=-1, keepdims=True)
    xc = xf - mu
    var = jnp.mean(xc * xc, axis=-1, keepdims=True)
    y = xc * lax.rsqrt(var + LN_EPS) * g.astype(jnp.float32) + b.astype(jnp.float32)
    return y.astype(x.dtype)


def gla_chunked(q, k, v, logf):
    B_, S_, H, Dh = q.shape
    n = S_ // CHUNK

    def to_chunks(a):
        return a.reshape(B_, n, CHUNK, H, Dh).transpose(1, 0, 3, 2, 4)

    qc, kc, vc, gc = to_chunks(q), to_chunks(k), to_chunks(v), to_chunks(logf)
    gc = jnp.cumsum(gc, axis=3)
    causal_in_chunk = jnp.tril(jnp.ones((CHUNK, CHUNK), dtype=bool))[:, :, None]

    def step(state, inp):
        qb, kb, vb, gb = inp
        o_inter = jnp.einsum('bhtk,bhkv->bhtv', qb * jnp.exp(gb), state)
        diff = gb[:, :, :, None, :] - gb[:, :, None, :, :]
        decay = jnp.where(causal_in_chunk, jnp.exp(jnp.where(causal_in_chunk, diff, 0.0)), 0.0)
        att = jnp.einsum('bhtk,bhsk,bhtsk->bhts', qb, kb, decay)
        o = o_inter + jnp.einsum('bhts,bhsv->bhtv', att, vb)
        g_last = gb[:, :, -1, :]
        k_dec = kb * jnp.exp(g_last[:, :, None, :] - gb)
        state = state * jnp.exp(g_last)[..., None] + jnp.einsum('bhsk,bhsv->bhkv', k_dec, vb)
        return state, o

    state0 = jnp.zeros((B_, H, Dh, Dh), jnp.float32)
    _, o = lax.scan(step, state0, (qc, kc, vc, gc))
    return o.transpose(1, 0, 3, 2, 4).reshape(B_, S_, H, Dh)


def hgrn2_mixer(q_raw, i_raw, g_raw, f_fw_raw, f_bw_raw, lb, norm_g):
    B_, S_, _ = q_raw.shape

    def heads(a):
        return a.astype(jnp.float32).reshape(B_, S_, A_HEADS, A_HEAD_DIM)

    q = heads(jax.nn.silu(q_raw.astype(jnp.float32)))
    v = heads(i_raw)
    lb = lb.astype(jnp.float32)

    def forget(f_raw, lb_dir):
        f = lb_dir + (1.0 - lb_dir) * jax.nn.sigmoid(f_raw.astype(jnp.float32))
        f = jnp.clip(f, F_MIN, 1.0)
        return heads(f)

    f_fw = forget(f_fw_raw, lb[0])
    f_bw = forget(f_bw_raw, lb[1])
    o_fw = gla_chunked(q, 1.0 - f_fw, v, jnp.log(f_fw))
    flip = lambda a: jnp.flip(a, axis=1)
    o_bw = flip(gla_chunked(flip(q), flip(1.0 - f_bw), flip(v), flip(jnp.log(f_bw))))
    o = o_fw + o_bw
    o = o * lax.rsqrt(jnp.mean(o * o, axis=-1, keepdims=True) + RMS_EPS)
    o = o.reshape(B_, S_, A_WIDTH) * norm_g.astype(jnp.float32) * jax.nn.silu(g_raw.astype(jnp.float32))
    return o.astype(q_raw.dtype)


def neighbourhood_attention(q, k, v, rpb):
    B_, S_, _ = q.shape
    rows = S_ // GRID_W
    wr = min(WIN_R_MAX, rows)

    def grid(a):
        return a.astype(jnp.float32).reshape(B_, rows, GRID_W, NA_HEADS, NA_HEAD_DIM)

    qg, kg, vg = grid(q), grid(k), grid(v)
    col = jnp.arange(GRID_W)
    col_start = jnp.clip(col - WIN_C // 2, 0, GRID_W - WIN_C)
    col_idx = col_start[:, None] + jnp.arange(WIN_C)[None, :]
    dc = col_idx - col[:, None]
    rpb_c = rpb.astype(jnp.float32)[:, :, dc + WIN_C - 1]
    scale = NA_HEAD_DIM ** -0.5

    def row_block(r):
        rs = jnp.clip(r - WIN_R_MAX // 2, 0, rows - wr)
        k_rows = lax.dynamic_slice_in_dim(kg, rs, wr, axis=1)
        v_rows = lax.dynamic_slice_in_dim(vg, rs, wr, axis=1)
        k_win = k_rows[:, :, col_idx]
        v_win = v_rows[:, :, col_idx]
        q_row = lax.dynamic_index_in_dim(qg, r, axis=1, keepdims=False)
        dr = rs + jnp.arange(wr) - r
        bias = rpb_c[:, dr + WIN_R_MAX - 1].transpose(0, 2, 1, 3)
        s = jnp.einsum('bchd,brcjhd->bhcrj', q_row, k_win) * scale + bias[None]
        prob = jax.nn.softmax(s.reshape(B_, NA_HEADS, GRID_W, wr * WIN_C), axis=-1)
        prob = prob.reshape(B_, NA_HEADS, GRID_W, wr, WIN_C)
        return jnp.einsum('bhcrj,brcjhd->bchd', prob, v_win)

    out = lax.map(row_block, jnp.arange(rows))
    return out.transpose(1, 0, 2, 3, 4).reshape(B_, S_, NA_WIDTH).astype(q.dtype)


def setup_inputs(seed: int = 0) -> dict:
    key = jax.random.key(seed)
    ks = jax.random.split(key, 17)

    def nrm(k, shape, scale):
        return jax.random.normal(k, shape, jnp.float32) * scale

    return {
        "x": nrm(ks[0], (BATCH, SEQ, D_MODEL), 1.0),
        "p": nrm(ks[1], (DEPTH, BATCH, SEQ, PLE_DIM), 1.0),
        "w_in": nrm(ks[2], (DEPTH, D_MODEL, N_IN), D_MODEL ** -0.5),
        "b_in": nrm(ks[3], (DEPTH, N_IN), 0.02),
        "lb_logits": nrm(ks[4], (DEPTH, 2, A_WIDTH), 1.0),
        "a_norm_g": 1.0 + nrm(ks[5], (DEPTH, A_WIDTH), 0.01),
        "rpb": nrm(ks[6], (DEPTH, NA_HEADS, 2 * WIN_R_MAX - 1, 2 * WIN_C - 1), 0.02),
        "w_branch": nrm(ks[7], (DEPTH, 2, A_WIDTH, D_MODEL), A_WIDTH ** -0.5),
        "w_out": nrm(ks[8], (DEPTH, D_MODEL, D_MODEL), DEEPNORM_BETA * D_MODEL ** -0.5),
        "ln1_g": 1.0 + nrm(ks[9], (DEPTH, D_MODEL), 0.01),
        "ln1_b": nrm(ks[10], (DEPTH, D_MODEL), 0.01),
        "w_ffn_up": nrm(ks[11], (DEPTH, D_MODEL, 2 * FFN_HIDDEN), D_MODEL ** -0.5),
        "w_ffn_down": nrm(ks[12], (DEPTH, FFN_HIDDEN, D_MODEL), DEEPNORM_BETA * FFN_HIDDEN ** -0.5),
        "w_pe": nrm(ks[13], (DEPTH, PLE_DIM, D_MODEL), DEEPNORM_BETA * PLE_DIM ** -0.5),
        "w_pg": nrm(ks[14], (DEPTH, D_MODEL, D_MODEL), D_MODEL ** -0.5),
        "ln2_g": 1.0 + nrm(ks[15], (DEPTH, D_MODEL), 0.01),
        "ln2_b": nrm(ks[16], (DEPTH, D_MODEL), 0.01),
    }


def reference(x, p, w_in, b_in, lb_logits, a_norm_g, rpb, w_branch, w_out, ln1_g, ln1_b,
              w_ffn_up, w_ffn_down, w_pe, w_pg, ln2_g, ln2_b):
    split_points = [int(c) for c in np.cumsum(SPLIT_SIZES)[:-1]]
    lb_sm = jax.nn.softmax(lb_logits.astype(jnp.float32), axis=0)
    lower_bounds = jnp.cumsum(lb_sm, axis=0) - lb_sm[0:1]

    for l in range(DEPTH):
        proj = jnp.einsum('bsd,de->bse', x, w_in[l]) + b_in[l]
        qa, ia, ga, f_fw, f_bw, qb, kb, vb, gate_a, gate_b = jnp.split(proj, split_points, axis=-1)
        oa = hgrn2_mixer(qa, ia, ga, f_fw, f_bw, lower_bounds[l], a_norm_g[l])
        ob = neighbourhood_attention(qb, kb, vb, rpb[l])
        merged = (jax.nn.sigmoid(gate_a) * jnp.einsum('bse,ed->bsd', oa, w_branch[l, 0])
                  + jax.nn.sigmoid(gate_b) * jnp.einsum('bse,ed->bsd', ob, w_branch[l, 1]))
        mix = jnp.einsum('bsd,de->bse', merged, w_out[l])
        x = layer_norm(DEEPNORM_ALPHA * x + mix, ln1_g[l], ln1_b[l])
        up = jnp.einsum('bsd,df->bsf', x, w_ffn_up[l])
        u_gate, u_val = jnp.split(up, 2, axis=-1)
        ffn = jnp.einsum('bsf,fd->bsd', jax.nn.silu(u_gate) * u_val, w_ffn_down[l])
        ple = jax.nn.sigmoid(jnp.einsum('bsd,de->bse', x, w_pg[l])) * jnp.einsum('bsk,kd->bsd', p[l], w_pe[l])
        x = layer_norm(DEEPNORM_ALPHA * x + ffn + ple, ln2_g[l], ln2_b[l])
    return x
```

```python
import functools

import jax
import jax.numpy as jnp
from jax import lax
from jax.experimental import pallas as pl
from jax.experimental.pallas import tpu as pltpu

F32 = jnp.float32
BF16 = jnp.bfloat16

HEAD_DIM = 128
GRID_W = 64
WIN_R = 8
WIN_C = 16
ROW_BLOCK = 8
F_MIN = 1e-6
LN_EPS = 1e-5
RMS_EPS = 1e-6
MASK_NEG = -1e30

VMEM_LIMIT = 56 * 1024 * 1024

_NT = (((1,), (1,)), ((), ()))
_TN = (((0,), (0,)), ((), ()))


def _silu(x):
    return x * jax.nn.sigmoid(x)


def _layer_norm(y, g, b):
    mu = jnp.mean(y, axis=-1, keepdims=True)
    yc = y - mu
    var = jnp.mean(yc * yc, axis=-1, keepdims=True)
    return yc * lax.rsqrt(var + LN_EPS) * g + b


def _params(*sem):
    return pltpu.CompilerParams(dimension_semantics=sem, vmem_limit_bytes=VMEM_LIMIT)


def _proj_kernel(x_ref, w_ref, b_ref, o_ref, xb_ref):
    @pl.when(pl.program_id(1) == 0)
    def _():
        xb_ref[...] = x_ref[...].astype(BF16)

    o_ref[...] = jnp.dot(xb_ref[...], w_ref[...], preferred_element_type=F32) + b_ref[...]


def _proj(x, w, b, tm=1024, tn=1024):
    n, d = x.shape
    n_in = w.shape[1]
    return pl.pallas_call(
        _proj_kernel,
        grid=(n // tm, n_in // tn),
        in_specs=[
            pl.BlockSpec((tm, d), lambda i, j: (i, 0)),
            pl.BlockSpec((d, tn), lambda i, j: (0, j)),
            pl.BlockSpec((1, tn), lambda i, j: (0, j)),
        ],
        out_specs=pl.BlockSpec((tm, tn), lambda i, j: (i, j)),
        out_shape=jax.ShapeDtypeStruct((n, n_in), F32),
        scratch_shapes=[pltpu.VMEM((tm, d), BF16)],
        compiler_params=_params("arbitrary", "arbitrary"),
        name="proj",
    )(x, w, b)


def _hgrn_kernel(*refs, reverse, final, chunk, heads):
    if final:
        q_ref, v_ref, f_ref, lb_ref, ga_ref, ng_ref, op_ref, o_ref, st_ref, g_scr = refs
    else:
        q_ref, v_ref, f_ref, lb_ref, o_ref, st_ref, g_scr = refs
    c_len = chunk
    width = q_ref.shape[1]

    @pl.when(pl.program_id(1) == 0)
    def _():
        st_ref[...] = jnp.zeros_like(st_ref)

    lb = lb_ref[...]
    f = lb + (1.0 - lb) * jax.nn.sigmoid(f_ref[...])
    f = jnp.clip(f, F_MIN, 1.0)
    kk = 1.0 - f

    row = lax.broadcasted_iota(jnp.int32, (c_len, 1), 0)
    g = jnp.log(f)
    k = 1
    while k < c_len:
        if reverse:
            g = g + jnp.where(row < c_len - k, pltpu.roll(g, c_len - k, axis=0), 0.0)
        else:
            g = g + jnp.where(row >= k, pltpu.roll(g, k, axis=0), 0.0)
        k *= 2
    g_scr[...] = g

    def ref_rows(block, offset):
        parts = [
            jnp.broadcast_to(g_scr[pl.ds(b * block + offset, 1), :], (block, width))
            for b in range(c_len // block)
        ]
        return parts[0] if len(parts) == 1 else jnp.concatenate(parts, axis=0)

    qh = _silu(q_ref[...])
    vb = v_ref[...].astype(BF16)

    t_idx = lax.broadcasted_iota(jnp.int32, (c_len, c_len), 0)
    s_idx = lax.broadcasted_iota(jnp.int32, (c_len, c_len), 1)
    causal = (s_idx >= t_idx) if reverse else (s_idx <= t_idx)
    txs = t_idx ^ s_idx

    q_lv, k_lv, masks = [], [], []
    c = c_len // 2
    while c >= 8:
        gref = ref_rows(2 * c, c if reverse else c - 1)
        e = jnp.exp(-jnp.abs(g - gref))
        q_lv.append((qh * e).astype(BF16))
        k_lv.append((kk * e).astype(BF16))
        shift = c.bit_length() - 1
        masks.append(jnp.logical_and((txs >> shift) == 1, causal))
        c //= 2
    gref = ref_rows(8, 4 if reverse else 3)
    q_lv.append((qh * jnp.exp(g - gref)).astype(BF16))
    k_lv.append((kk * jnp.exp(gref - g)).astype(BF16))
    masks.append(jnp.logical_and((txs >> 3) == 0, causal))

    g_last = g_scr[pl.ds(0 if reverse else c_len - 1, 1), :]
    q_in = (qh * jnp.exp(g)).astype(BF16)
    k_dec = (kk * jnp.exp(g_last - g)).astype(BF16)
    d_last = jnp.exp(g_last)

    if final:
        gate = ng_ref[...] * _silu(ga_ref[...])

    for h in range(heads):
        hs = slice(h * HEAD_DIM, (h + 1) * HEAD_DIM)
        att = None
        for ql, kl, m in zip(q_lv, k_lv, masks):
            sc = lax.dot_general(ql[:, hs], kl[:, hs], _NT, preferred_element_type=F32)
            att = jnp.where(m, sc, 0.0 if att is None else att)
        st = st_ref[h]
        o = lax.dot_general(q_in[:, hs], st.astype(BF16), _NT, preferred_element_type=F32)
        o = o + jnp.dot(att.astype(BF16), vb[:, hs], preferred_element_type=F32)
        st_ref[h] = st * d_last[:, hs] + lax.dot_general(
            vb[:, hs], k_dec[:, hs], _TN, preferred_element_type=F32)
        if final:
            o = o + op_ref[:, hs]
            ms = jnp.mean(o * o, axis=-1, keepdims=True)
            o_ref[:, hs] = (o * lax.rsqrt(ms + RMS_EPS) * gate[:, hs]).astype(o_ref.dtype)
        else:
            o_ref[:, hs] = o


def _hgrn(proj, lb, batch, *, reverse, col_f, norm_g=None, o_prev=None, chunk=128):
    n = proj.shape[0]
    width = lb.shape[-1]
    heads = width // HEAD_DIM
    nc = n // batch // chunk
    final = o_prev is not None

    def tok(b, i):
        return b * nc + ((nc - 1 - i) if reverse else i)

    def col(j):
        return pl.BlockSpec((chunk, width), lambda b, i: (tok(b, i), j))

    vec = pl.BlockSpec((1, width), lambda b, i: (0, 0))
    in_specs = [col(0), col(1), col(col_f), vec]
    args = [proj, proj, proj, lb.reshape(1, width)]
    if final:
        in_specs += [col(2), vec, col(0)]
        args += [proj, norm_g.reshape(1, width), o_prev]
    return pl.pallas_call(
        functools.partial(_hgrn_kernel, reverse=reverse, final=final, chunk=chunk, heads=heads),
        grid=(batch, nc),
        in_specs=in_specs,
        out_specs=col(0),
        out_shape=jax.ShapeDtypeStruct((n, width), BF16 if final else F32),
        scratch_shapes=[
            pltpu.VMEM((heads, HEAD_DIM, HEAD_DIM), F32),
            pltpu.VMEM((chunk, width), F32),
        ],
        compiler_params=_params("arbitrary", "arbitrary"),
        name="hgrn_bwd" if reverse else "hgrn_fwd",
    )(*args)


def _na_table(rpb):
    half = WIN_R // 2
    qr = jnp.arange(ROW_BLOCK)[:, None, None, None]
    qc = jnp.arange(GRID_W)[None, :, None, None]
    kr = jnp.arange(2 * ROW_BLOCK)[None, None, :, None] - half
    kc = jnp.arange(GRID_W)[None, None, None, :]
    cs = jnp.clip(qc - WIN_C // 2, 0, GRID_W - WIN_C)
    col_ok = (kc >= cs) & (kc < cs + WIN_C)
    dc = jnp.clip(kc - qc + WIN_C - 1, 0, 2 * WIN_C - 2)
    dr = jnp.clip(kr - qr + WIN_R - 1, 0, 2 * WIN_R - 2)
    tabs = []
    for rs in (jnp.maximum(qr - half, 0), qr - half, jnp.minimum(qr - half, 0)):
        ok = (kr >= rs) & (kr < rs + WIN_R) & col_ok
        bias = rpb.astype(F32)[:, dr, dc]
        tabs.append(jnp.where(ok[None], bias, MASK_NEG))
    nq = ROW_BLOCK * GRID_W
    return jnp.stack(tabs).reshape(3, rpb.shape[0], nq, 2 * nq)


def _na_kernel(q_ref, kp_ref, kc_ref, kn_ref, vp_ref, vc_ref, vn_ref, tab_ref, o_ref, *, scale):
    half = q_ref.shape[0] // 2

    def window(prev, cur, nxt):
        return jnp.concatenate([prev[half:, :], cur[...], nxt[:half, :]], axis=0).astype(BF16)

    q = q_ref[...].astype(BF16)
    k = window(kp_ref, kc_ref, kn_ref)
    v = window(vp_ref, vc_ref, vn_ref)
    s = lax.dot_general(q, k, _NT, preferred_element_type=F32) * scale + tab_ref[...]
    m = jnp.max(s, axis=-1, keepdims=True)
    p = jnp.exp(s - m)
    l = jnp.sum(p, axis=-1, keepdims=True)
    o = jnp.dot(p.astype(BF16), v, preferred_element_type=F32) / l
    o_ref[...] = o.astype(o_ref.dtype)


def _na(proj, table, batch, col_q):
    n = proj.shape[0]
    heads = table.shape[1]
    nq = ROW_BLOCK * GRID_W
    nblk = n // batch // nq
    assert nblk >= 2
    cq = col_q // HEAD_DIM

    def spec(which, shift):
        def index(h, b, i):
            return (b * nblk + jnp.clip(i + shift, 0, nblk - 1), cq + which * heads + h)
        return pl.BlockSpec((nq, HEAD_DIM), index)

    def tab_index(h, b, i):
        return (jnp.where(i == 0, 0, jnp.where(i == nblk - 1, 2, 1)), h, 0, 0)

    return pl.pallas_call(
        functools.partial(_na_kernel, scale=HEAD_DIM ** -0.5),
        grid=(heads, batch, nblk),
        in_specs=[spec(0, 0), spec(1, -1), spec(1, 0), spec(1, 1),
                  spec(2, -1), spec(2, 0), spec(2, 1),
                  pl.BlockSpec((None, None, nq, 2 * nq), tab_index)],
        out_specs=pl.BlockSpec((nq, HEAD_DIM), lambda h, b, i: (b * nblk + i, h)),
        out_shape=jax.ShapeDtypeStruct((n, heads * HEAD_DIM), BF16),
        compiler_params=_params("arbitrary", "arbitrary", "arbitrary"),
        name="natten",
    )(proj, proj, proj, proj, proj, proj, proj, table)


def _mix_kernel(oa_ref, ob_ref, ga_ref, gb_ref, x_ref, wb_ref, wo_ref, g_ref, b_ref, o_ref, *, alpha):
    ya = jnp.dot(oa_ref[...], wb_ref[0], preferred_element_type=F32)
    yb = jnp.dot(ob_ref[...], wb_ref[1], preferred_element_type=F32)
    merged = jax.nn.sigmoid(ga_ref[...]) * ya + jax.nn.sigmoid(gb_ref[...]) * yb
    mix = jnp.dot(merged.astype(BF16), wo_ref[...], preferred_element_type=F32)
    o_ref[...] = _layer_norm(alpha * x_ref[...] + mix, g_ref[...], b_ref[...])


def _mix(oa, ob, proj, x, w_branch, w_out, ln_g, ln_b, col_gate, alpha, tm=256):
    n, d = x.shape
    wa = oa.shape[1]
    cg = col_gate // d
    const = pl.Buffered(1)
    return pl.pallas_call(
        functools.partial(_mix_kernel, alpha=alpha),
        grid=(n // tm,),
        in_specs=[
            pl.BlockSpec((tm, wa), lambda i: (i, 0)),
            pl.BlockSpec((tm, wa), lambda i: (i, 0)),
            pl.BlockSpec((tm, d), lambda i: (i, cg)),
            pl.BlockSpec((tm, d), lambda i: (i, cg + 1)),
            pl.BlockSpec((tm, d), lambda i: (i, 0)),
            pl.BlockSpec((2, wa, d), lambda i: (0, 0, 0), pipeline_mode=const),
            pl.BlockSpec((d, d), lambda i: (0, 0), pipeline_mode=const),
            pl.BlockSpec((1, d), lambda i: (0, 0)),
            pl.BlockSpec((1, d), lambda i: (0, 0)),
        ],
        out_specs=pl.BlockSpec((tm, d), lambda i: (i, 0)),
        out_shape=jax.ShapeDtypeStruct((n, d), F32),
        compiler_params=_params("arbitrary"),
        name="mix",
    )(oa, ob, proj, proj, x, w_branch, w_out, ln_g.reshape(1, d), ln_b.reshape(1, d))


def _ffn_kernel(x_ref, p_ref, wg_ref, wv_ref, wd_ref, wpg_ref, wpe_ref, g_ref, b_ref, o_ref,
                xb_ref, acc_ref, *, alpha):
    j = pl.program_id(1)

    @pl.when(j == 0)
    def _():
        xb = x_ref[...].astype(BF16)
        xb_ref[...] = xb
        pg = jnp.dot(xb, wpg_ref[...], preferred_element_type=F32)
        pe = jnp.dot(p_ref[...].astype(BF16), wpe_ref[...], preferred_element_type=F32)
        acc_ref[...] = jax.nn.sigmoid(pg) * pe

    xb = xb_ref[...]
    ug = jnp.dot(xb, wg_ref[...], preferred_element_type=F32)
    uv = jnp.dot(xb, wv_ref[...], preferred_element_type=F32)
    hid = (_silu(ug) * uv).astype(BF16)
    acc_ref[...] += jnp.dot(hid, wd_ref[...], preferred_element_type=F32)

    @pl.when(j == pl.num_programs(1) - 1)
    def _():
        o_ref[...] = _layer_norm(alpha * x_ref[...] + acc_ref[...], g_ref[...], b_ref[...])


def _ffn(x, p, w_up, w_down, w_pg, w_pe, ln_g, ln_b, alpha, tm=512, th=256):
    n, d = x.shape
    hidden = w_down.shape[0]
    nh = hidden // th
    const = pl.Buffered(1)
    return pl.pallas_call(
        functools.partial(_ffn_kernel, alpha=alpha),
        grid=(n // tm, nh),
        in_specs=[
            pl.BlockSpec((tm, d), lambda i, j: (i, 0)),
            pl.BlockSpec((tm, p.shape[1]), lambda i, j: (i, 0)),
            pl.BlockSpec((d, th), lambda i, j: (0, j)),
            pl.BlockSpec((d, th), lambda i, j: (0, nh + j)),
            pl.BlockSpec((th, d), lambda i, j: (j, 0)),
            pl.BlockSpec((d, d), lambda i, j: (0, 0), pipeline_mode=const),
            pl.BlockSpec(w_pe.shape, lambda i, j: (0, 0), pipeline_mode=const),
            pl.BlockSpec((1, d), lambda i, j: (0, 0)),
            pl.BlockSpec((1, d), lambda i, j: (0, 0)),
        ],
        out_specs=pl.BlockSpec((tm, d), lambda i, j: (i, 0)),
        out_shape=jax.ShapeDtypeStruct((n, d), F32),
        scratch_shapes=[pltpu.VMEM((tm, d), BF16), pltpu.VMEM((tm, d), F32)],
        compiler_params=_params("arbitrary", "arbitrary"),
        name="ffn",
    )(x, p, w_up, w_up, w_down, w_pg, w_pe, ln_g.reshape(1, d), ln_b.reshape(1, d))


def kernel(x, p, w_in, b_in, lb_logits, a_norm_g, rpb, w_branch, w_out, ln1_g, ln1_b,
           w_ffn_up, w_ffn_down, w_pe, w_pg, ln2_g, ln2_b):
    batch, seq, d = x.shape
    depth = w_in.shape[0]
    n = batch * seq
    a_width = lb_logits.shape[-1]
    na_width = rpb.shape[1] * HEAD_DIM
    alpha = (2 * depth) ** 0.25
    col_f = 3
    col_q = 5 * a_width
    col_gate = 5 * a_width + 3 * na_width
    assert col_gate % d == 0 and (seq // GRID_W) % ROW_BLOCK == 0

    lb_sm = jax.nn.softmax(lb_logits.astype(F32), axis=0)
    lower = jnp.cumsum(lb_sm, axis=0) - lb_sm[0:1]

    xf = x.reshape(n, d)
    for l in range(depth):
        proj = _proj(xf, w_in[l].astype(BF16), b_in[l].reshape(1, -1))
        o_fw = _hgrn(proj, lower[l, 0], batch, reverse=False, col_f=col_f)
        oa = _hgrn(proj, lower[l, 1], batch, reverse=True, col_f=col_f + 1,
                   norm_g=a_norm_g[l], o_prev=o_fw)
        ob = _na(proj, _na_table(rpb[l]), batch, col_q)
        x1 = _mix(oa, ob, proj, xf, w_branch[l].astype(BF16), w_out[l].astype(BF16),
                  ln1_g[l], ln1_b[l], col_gate, alpha)
        xf = _ffn(x1, p[l].reshape(n, -1), w_ffn_up[l].astype(BF16), w_ffn_down[l].astype(BF16),
                  w_pg[l].astype(BF16), w_pe[l].astype(BF16), ln2_g[l], ln2_b[l], alpha)
    return xf.reshape(batch, seq, d)
```

```python
import functools

import jax
import jax.numpy as jnp
from jax import lax
from jax.experimental import pallas as pl
from jax.experimental.pallas import tpu as pltpu

F32 = jnp.float32
BF16 = jnp.bfloat16

HEAD_DIM = 128
GRID_W = 64
WIN_R = 8
WIN_C = 16
ROW_BLOCK = 8
F_MIN = 1e-6
LN_EPS = 1e-5
RMS_EPS = 1e-6
MASK_NEG = -1e30

VMEM_LIMIT = 56 * 1024 * 1024

_NT = (((1,), (1,)), ((), ()))
_TN = (((0,), (0,)), ((), ()))


def _silu(x):
    return x * jax.nn.sigmoid(x)


def _layer_norm(y, g, b):
    mu = jnp.mean(y, axis=-1, keepdims=True)
    yc = y - mu
    var = jnp.mean(yc * yc, axis=-1, keepdims=True)
    return yc * lax.rsqrt(var + LN_EPS) * g + b


def _params(*sem):
    return pltpu.CompilerParams(dimension_semantics=sem, vmem_limit_bytes=VMEM_LIMIT)


def _proj_kernel(x_ref, w_ref, b_ref, o_ref, xb_ref):
    @pl.when(pl.program_id(1) == 0)
    def _():
        xb_ref[...] = x_ref[...].astype(BF16)

    o_ref[...] = jnp.dot(xb_ref[...], w_ref[...], preferred_element_type=F32) + b_ref[...]


def _proj(x, w, b, tm=1024, tn=1024):
    n, d = x.shape
    n_in = w.shape[1]
    return pl.pallas_call(
        _proj_kernel,
        grid=(n // tm, n_in // tn),
        in_specs=[
            pl.BlockSpec((tm, d), lambda i, j: (i, 0)),
            pl.BlockSpec((d, tn), lambda i, j: (0, j)),
            pl.BlockSpec((1, tn), lambda i, j: (0, j)),
        ],
        out_specs=pl.BlockSpec((tm, tn), lambda i, j: (i, j)),
        out_shape=jax.ShapeDtypeStruct((n, n_in), F32),
        scratch_shapes=[pltpu.VMEM((tm, d), BF16)],
        compiler_params=_params("arbitrary", "arbitrary"),
        name="proj",
    )(x, w, b)


def _hgrn_kernel(*refs, reverse, final, chunk, heads):
    if final:
        q_ref, v_ref, f_ref, lb_ref, ga_ref, ng_ref, op_ref, o_ref, st_ref, g_scr = refs
    else:
        q_ref, v_ref, f_ref, lb_ref, o_ref, st_ref, g_scr = refs
    c_len = chunk
    width = q_ref.shape[1]

    @pl.when(pl.program_id(1) == 0)
    def _():
        st_ref[...] = jnp.zeros_like(st_ref)

    row = lax.broadcasted_iota(jnp.int32, (c_len, 1), 0)
    t_idx = lax.broadcasted_iota(jnp.int32, (c_len, c_len), 0)
    s_idx = lax.broadcasted_iota(jnp.int32, (c_len, c_len), 1)
    causal = (s_idx >= t_idx) if reverse else (s_idx <= t_idx)
    txs = t_idx ^ s_idx
    mask_diag = jnp.logical_and((txs >> 3) == 0, causal)
    pair_sizes = [c_len >> i for i in range(1, c_len.bit_length() - 3)]
    pair_masks = [jnp.logical_and((txs >> (c.bit_length() - 1)) == 1, causal) for c in pair_sizes]
    is_query = [((row & c) == 0) if reverse else ((row & c) != 0) for c in pair_sizes]

    for h in range(heads):
        hs = slice(h * HEAD_DIM, (h + 1) * HEAD_DIM)
        lb = lb_ref[:, hs]
        f = lb + (1.0 - lb) * jax.nn.sigmoid(f_ref[:, hs])
        f = jnp.clip(f, F_MIN, 1.0)
        kk = 1.0 - f

        g = jnp.log2(f)
        k = 1
        while k < c_len:
            if reverse:
                g = g + jnp.where(row < c_len - k, pltpu.roll(g, c_len - k, axis=0), 0.0)
            else:
                g = g + jnp.where(row >= k, pltpu.roll(g, k, axis=0), 0.0)
            k *= 2
        g_scr[:, hs] = g

        def ref_rows(block, offset):
            parts = [
                jnp.broadcast_to(g_scr[pl.ds(b * block + offset, 1), hs], (block, HEAD_DIM))
                for b in range(c_len // block)
            ]
            return parts[0] if len(parts) == 1 else jnp.concatenate(parts, axis=0)

        qh = _silu(q_ref[:, hs])
        vb = v_ref[:, hs].astype(BF16)

        gref = ref_rows(8, 4 if reverse else 3)
        xq = (qh * jnp.exp2(g - gref)).astype(BF16)
        xk = (kk * jnp.exp2(gref - g)).astype(BF16)
        att = jnp.where(mask_diag,
                        lax.dot_general(xq, xk, _NT, preferred_element_type=F32), 0.0)
        for c, m, isq in zip(pair_sizes, pair_masks, is_query):
            gref = ref_rows(2 * c, c if reverse else c - 1)
            xl = (jnp.where(isq, qh, kk) * jnp.exp2(-jnp.abs(g - gref))).astype(BF16)
            att = jnp.where(m, lax.dot_general(xl, xl, _NT, preferred_element_type=F32), att)

        g_last = g_scr[pl.ds(0 if reverse else c_len - 1, 1), hs]
        q_in = (qh * jnp.exp2(g)).astype(BF16)
        k_dec = (kk * jnp.exp2(g_last - g)).astype(BF16)
        st = st_ref[h]
        vt = vb.T
        o = lax.dot_general(jnp.concatenate([q_in, att.astype(BF16)], axis=1),
                            jnp.concatenate([st.astype(BF16), vt], axis=1),
                            _NT, preferred_element_type=F32)
        st_ref[h] = st * jnp.exp2(g_last) + jnp.dot(vt, k_dec, preferred_element_type=F32)
        if final:
            o = o + op_ref[:, hs]
            ms = jnp.mean(o * o, axis=-1, keepdims=True)
            gate = ng_ref[:, hs] * _silu(ga_ref[:, hs])
            o_ref[:, hs] = (o * lax.rsqrt(ms + RMS_EPS) * gate).astype(o_ref.dtype)
        else:
            o_ref[:, hs] = o


def _hgrn(proj, lb, batch, *, reverse, col_f, norm_g=None, o_prev=None, chunk=128):
    n = proj.shape[0]
    width = lb.shape[-1]
    heads = width // HEAD_DIM
    nc = n // batch // chunk
    final = o_prev is not None

    def tok(b, i):
        return b * nc + ((nc - 1 - i) if reverse else i)

    def col(j):
        return pl.BlockSpec((chunk, width), lambda b, i: (tok(b, i), j))

    vec = pl.BlockSpec((1, width), lambda b, i: (0, 0))
    in_specs = [col(0), col(1), col(col_f), vec]
    args = [proj, proj, proj, lb.reshape(1, width)]
    if final:
        in_specs += [col(2), vec, col(0)]
        args += [proj, norm_g.reshape(1, width), o_prev]
    return pl.pallas_call(
        functools.partial(_hgrn_kernel, reverse=reverse, final=final, chunk=chunk, heads=heads),
        grid=(batch, nc),
        in_specs=in_specs,
        out_specs=col(0),
        out_shape=jax.ShapeDtypeStruct((n, width), BF16 if final else F32),
        scratch_shapes=[
            pltpu.VMEM((heads, HEAD_DIM, HEAD_DIM), F32),
            pltpu.VMEM((chunk, width), F32),
        ],
        compiler_params=_params("arbitrary", "arbitrary"),
        name="hgrn_bwd" if reverse else "hgrn_fwd",
    )(*args)


HALF_ROWS = ROW_BLOCK // 2
KEY_ROWS = HALF_ROWS + WIN_R


def _na_table(rpb):
    hp = lax.Precision.HIGHEST
    nh = rpb.shape[0]
    qr4 = jnp.arange(HALF_ROWS)
    kr12 = jnp.arange(KEY_ROWS)
    col = jnp.arange(GRID_W)
    dr = kr12[None, :] - qr4[:, None] + (WIN_R - 1 - HALF_ROWS)
    dc = col[None, :] - col[:, None] + (WIN_C - 1)
    oh_r = (dr[..., None] == jnp.arange(2 * WIN_R - 1)).astype(F32)
    oh_c = (dc[..., None] == jnp.arange(2 * WIN_C - 1)).astype(F32)
    t1 = jnp.einsum('hab,rka->hrkb', rpb.astype(F32), oh_r, precision=hp)
    bias = jnp.einsum('hrkb,cjb->hrckj', t1, oh_c, precision=hp)

    a = jnp.arange(2)[:, None, None, None, None]
    qr = a * HALF_ROWS + qr4[None, :, None, None, None]
    kr = a * HALF_ROWS + kr12[None, None, None, :, None] - WIN_R // 2
    qc = col[None, None, :, None, None]
    kc = col[None, None, None, None, :]
    cs = jnp.clip(qc - WIN_C // 2, 0, GRID_W - WIN_C)
    col_ok = (kc >= cs) & (kc < cs + WIN_C)
    rs_mid = qr - WIN_R // 2
    tabs = []
    for rs in (jnp.maximum(rs_mid, 0), rs_mid, jnp.minimum(rs_mid, 0)):
        ok = (kr >= rs) & (kr < rs + WIN_R) & col_ok
        tabs.append(jnp.where(ok[None], bias[:, None], MASK_NEG))
    return jnp.stack(tabs).reshape(3, nh, ROW_BLOCK * GRID_W, KEY_ROWS * GRID_W)


def _na_kernel(q_ref, kp_ref, kc_ref, kn_ref, vp_ref, vc_ref, vn_ref, tab_ref, o_ref, *, scale):
    half = q_ref.shape[0] // 2
    nk = tab_ref.shape[2]
    for h in range(tab_ref.shape[0]):
        hs = slice(h * HEAD_DIM, (h + 1) * HEAD_DIM)

        def window(prev, cur, nxt):
            return jnp.concatenate([prev[half:, hs], cur[:, hs], nxt[:half, hs]],
                                   axis=0).astype(BF16)

        q = (q_ref[:, hs] * scale).astype(BF16)
        k = window(kp_ref, kc_ref, kn_ref)
        v = window(vp_ref, vc_ref, vn_ref)
        for a in range(2):
            qs = slice(a * half, (a + 1) * half)
            ks = slice(a * half, a * half + nk)
            s = lax.dot_general(q[qs], k[ks], _NT, preferred_element_type=F32) + tab_ref[h, qs, :]
            m = jnp.max(s, axis=-1, keepdims=True)
            p = jnp.exp(s - m)
            l = jnp.sum(p, axis=-1, keepdims=True)
            o = jnp.dot(p.astype(BF16), v[ks], preferred_element_type=F32) / l
            o_ref[qs, hs] = o.astype(o_ref.dtype)


def _na(proj, table, batch, col_q, heads_per_step=4):
    n = proj.shape[0]
    heads = table.shape[1]
    hp = heads_per_step
    nq = ROW_BLOCK * GRID_W
    nblk = n // batch // nq
    assert nblk >= 2 and heads % hp == 0
    cq = col_q // (hp * HEAD_DIM)
    hb = heads // hp

    def spec(which, shift):
        def index(h, b, i):
            return (b * nblk + jnp.clip(i + shift, 0, nblk - 1), cq + which * hb + h)
        return pl.BlockSpec((nq, hp * HEAD_DIM), index)

    def tab_index(h, b, i):
        return (jnp.where(i == 0, 0, jnp.where(i == nblk - 1, 2, 1)), h, 0, 0)

    return pl.pallas_call(
        functools.partial(_na_kernel, scale=HEAD_DIM ** -0.5),
        grid=(hb, batch, nblk),
        in_specs=[spec(0, 0), spec(1, -1), spec(1, 0), spec(1, 1),
                  spec(2, -1), spec(2, 0), spec(2, 1),
                  pl.BlockSpec((None, hp, nq, table.shape[3]), tab_index)],
        out_specs=pl.BlockSpec((nq, hp * HEAD_DIM), lambda h, b, i: (b * nblk + i, h)),
        out_shape=jax.ShapeDtypeStruct((n, heads * HEAD_DIM), BF16),
        compiler_params=_params("arbitrary", "arbitrary", "arbitrary"),
        name="natten",
    )(proj, proj, proj, proj, proj, proj, proj, table)


def _mix_kernel(oa_ref, ob_ref, ga_ref, gb_ref, x_ref, wb_ref, wo_ref, g_ref, b_ref, o_ref, *, alpha):
    ya = jnp.dot(oa_ref[...], wb_ref[0], preferred_element_type=F32)
    yb = jnp.dot(ob_ref[...], wb_ref[1], preferred_element_type=F32)
    merged = jax.nn.sigmoid(ga_ref[...]) * ya + jax.nn.sigmoid(gb_ref[...]) * yb
    mix = jnp.dot(merged.astype(BF16), wo_ref[...], preferred_element_type=F32)
    o_ref[...] = _layer_norm(alpha * x_ref[...] + mix, g_ref[...], b_ref[...])


def _mix(oa, ob, proj, x, w_branch, w_out, ln_g, ln_b, col_gate, alpha, tm=256):
    n, d = x.shape
    wa = oa.shape[1]
    cg = col_gate // d
    const = pl.Buffered(1)
    return pl.pallas_call(
        functools.partial(_mix_kernel, alpha=alpha),
        grid=(n // tm,),
        in_specs=[
            pl.BlockSpec((tm, wa), lambda i: (i, 0)),
            pl.BlockSpec((tm, wa), lambda i: (i, 0)),
            pl.BlockSpec((tm, d), lambda i: (i, cg)),
            pl.BlockSpec((tm, d), lambda i: (i, cg + 1)),
            pl.BlockSpec((tm, d), lambda i: (i, 0)),
            pl.BlockSpec((2, wa, d), lambda i: (0, 0, 0), pipeline_mode=const),
            pl.BlockSpec((d, d), lambda i: (0, 0), pipeline_mode=const),
            pl.BlockSpec((1, d), lambda i: (0, 0)),
            pl.BlockSpec((1, d), lambda i: (0, 0)),
        ],
        out_specs=pl.BlockSpec((tm, d), lambda i: (i, 0)),
        out_shape=jax.ShapeDtypeStruct((n, d), F32),
        compiler_params=_params("arbitrary"),
        name="mix",
    )(oa, ob, proj, proj, x, w_branch, w_out, ln_g.reshape(1, d), ln_b.reshape(1, d))


def _ffn_kernel(x_ref, p_ref, wg_ref, wv_ref, wd_ref, wpg_ref, wpe_ref, g_ref, b_ref, o_ref,
                xb_ref, acc_ref, *, alpha, nh):
    j = pl.program_id(1)
    tc = wpg_ref.shape[1]

    @pl.when(j == 0)
    def _():
        xb_ref[...] = x_ref[...].astype(BF16)
        acc_ref[...] = jnp.zeros_like(acc_ref)

    @pl.when(j < nh)
    def _():
        xb = xb_ref[...]
        ug = jnp.dot(xb, wg_ref[...], preferred_element_type=F32)
        uv = jnp.dot(xb, wv_ref[...], preferred_element_type=F32)
        hid = (_silu(ug) * uv).astype(BF16)
        acc_ref[...] += jnp.dot(hid, wd_ref[...], preferred_element_type=F32)

    @pl.when(j >= nh)
    def _():
        cols = pl.ds(pl.multiple_of((j - nh) * tc, tc), tc)
        pg = jnp.dot(xb_ref[...], wpg_ref[...], preferred_element_type=F32)
        pe = jnp.dot(p_ref[...].astype(BF16), wpe_ref[...], preferred_element_type=F32)
        acc_ref[:, cols] += jax.nn.sigmoid(pg) * pe

    @pl.when(j == pl.num_programs(1) - 1)
    def _():
        o_ref[...] = _layer_norm(alpha * x_ref[...] + acc_ref[...], g_ref[...], b_ref[...])


def _ffn(x, p, w_up, w_down, w_pg, w_pe, ln_g, ln_b, alpha, tm=512, th=512):
    n, d = x.shape
    hidden = w_down.shape[0]
    nh = hidden // th
    nc = d // th

    def hid(j):
        return jnp.minimum(j, nh - 1)

    def emb(j):
        return jnp.maximum(j - nh, 0)

    return pl.pallas_call(
        functools.partial(_ffn_kernel, alpha=alpha, nh=nh),
        grid=(n // tm, nh + nc),
        in_specs=[
            pl.BlockSpec((tm, d), lambda i, j: (i, 0)),
            pl.BlockSpec((tm, p.shape[1]), lambda i, j: (i, 0)),
            pl.BlockSpec((d, th), lambda i, j: (0, hid(j))),
            pl.BlockSpec((d, th), lambda i, j: (0, nh + hid(j))),
            pl.BlockSpec((th, d), lambda i, j: (hid(j), 0)),
            pl.BlockSpec((d, th), lambda i, j: (0, emb(j))),
            pl.BlockSpec((w_pe.shape[0], th), lambda i, j: (0, emb(j))),
            pl.BlockSpec((1, d), lambda i, j: (0, 0)),
            pl.BlockSpec((1, d), lambda i, j: (0, 0)),
        ],
        out_specs=pl.BlockSpec((tm, d), lambda i, j: (i, 0)),
        out_shape=jax.ShapeDtypeStruct((n, d), F32),
        scratch_shapes=[pltpu.VMEM((tm, d), BF16), pltpu.VMEM((tm, d), F32)],
        compiler_params=_params("arbitrary", "arbitrary"),
        name="ffn",
    )(x, p, w_up, w_up, w_down, w_pg, w_pe, ln_g.reshape(1, d), ln_b.reshape(1, d))


def kernel(x, p, w_in, b_in, lb_logits, a_norm_g, rpb, w_branch, w_out, ln1_g, ln1_b,
           w_ffn_up, w_ffn_down, w_pe, w_pg, ln2_g, ln2_b):
    batch, seq, d = x.shape
    depth = w_in.shape[0]
    n = batch * seq
    a_width = lb_logits.shape[-1]
    na_width = rpb.shape[1] * HEAD_DIM
    alpha = (2 * depth) ** 0.25
    col_f = 3
    col_q = 5 * a_width
    col_gate = 5 * a_width + 3 * na_width
    assert col_gate % d == 0 and (seq // GRID_W) % ROW_BLOCK == 0

    lb_sm = jax.nn.softmax(lb_logits.astype(F32), axis=0)
    lower = jnp.cumsum(lb_sm, axis=0) - lb_sm[0:1]

    xf = x.reshape(n, d)
    for l in range(depth):
        proj = _proj(xf, w_in[l].astype(BF16), b_in[l].reshape(1, -1))
        o_fw = _hgrn(proj, lower[l, 0], batch, reverse=False, col_f=col_f)
        oa = _hgrn(proj, lower[l, 1], batch, reverse=True, col_f=col_f + 1,
                   norm_g=a_norm_g[l], o_prev=o_fw)
        ob = _na(proj, _na_table(rpb[l]), batch, col_q)
        x1 = _mix(oa, ob, proj, xf, w_branch[l].astype(BF16), w_out[l].astype(BF16),
                  ln1_g[l], ln1_b[l], col_gate, alpha)
        xf = _ffn(x1, p[l].reshape(n, -1), w_ffn_up[l].astype(BF16), w_ffn_down[l].astype(BF16),
                  w_pg[l].astype(BF16), w_pe[l].astype(BF16), ln2_g[l], ln2_b[l], alpha)
    return xf.reshape(batch, seq, d)
```

```python
import functools

import jax
import jax.numpy as jnp
from jax import lax
from jax.experimental import pallas as pl
from jax.experimental.pallas import tpu as pltpu

F32 = jnp.float32
BF16 = jnp.bfloat16

HEAD_DIM = 128
GRID_W = 64
WIN_R = 8
WIN_C = 16
ROW_BLOCK = 8
F_MIN = 1e-6
LN_EPS = 1e-5
RMS_EPS = 1e-6
MASK_NEG = -1e30

VMEM_LIMIT = 60 * 1024 * 1024

_NT = (((1,), (1,)), ((), ()))
_TN = (((0,), (0,)), ((), ()))


def _silu(x):
    return x * jax.nn.sigmoid(x)


def _layer_norm(y, g, b):
    mu = jnp.mean(y, axis=-1, keepdims=True)
    yc = y - mu
    var = jnp.mean(yc * yc, axis=-1, keepdims=True)
    return yc * lax.rsqrt(var + LN_EPS) * g + b


def _params(*sem):
    return pltpu.CompilerParams(dimension_semantics=sem, vmem_limit_bytes=VMEM_LIMIT)


def _proj_kernel(x_ref, w_ref, b_ref, o_ref, xb_ref):
    @pl.when(pl.program_id(1) == 0)
    def _():
        xb_ref[...] = x_ref[...].astype(BF16)

    o_ref[...] = jnp.dot(xb_ref[...], w_ref[...], preferred_element_type=F32) + b_ref[...]


def _proj(x, w, b, tm=1024, tn=2048):
    n, d = x.shape
    n_in = w.shape[1]
    return pl.pallas_call(
        _proj_kernel,
        grid=(n // tm, n_in // tn),
        in_specs=[
            pl.BlockSpec((tm, d), lambda i, j: (i, 0)),
            pl.BlockSpec((d, tn), lambda i, j: (0, j)),
            pl.BlockSpec((1, tn), lambda i, j: (0, j)),
        ],
        out_specs=pl.BlockSpec((tm, tn), lambda i, j: (i, j)),
        out_shape=jax.ShapeDtypeStruct((n, n_in), F32),
        scratch_shapes=[pltpu.VMEM((tm, d), BF16)],
        compiler_params=_params("arbitrary", "arbitrary"),
        name="proj",
    )(x, w, b)


def _hgrn_kernel(*refs, reverse, final, chunk, heads):
    if final:
        q_ref, v_ref, f_ref, lb_ref, ga_ref, ng_ref, op_ref, o_ref, st_ref, g_scr = refs
    else:
        q_ref, v_ref, f_ref, lb_ref, o_ref, st_ref, g_scr = refs
    c_len = chunk
    width = q_ref.shape[1]

    @pl.when(pl.program_id(1) == 0)
    def _():
        st_ref[...] = jnp.zeros_like(st_ref)

    row = lax.broadcasted_iota(jnp.int32, (c_len, 1), 0)
    t_idx = lax.broadcasted_iota(jnp.int32, (c_len, c_len), 0)
    s_idx = lax.broadcasted_iota(jnp.int32, (c_len, c_len), 1)
    causal = (s_idx >= t_idx) if reverse else (s_idx <= t_idx)
    txs = t_idx ^ s_idx
    mask_diag = jnp.logical_and((txs >> 3) == 0, causal)
    pair_sizes = [c_len >> i for i in range(1, c_len.bit_length() - 3)]
    pair_masks = [jnp.logical_and((txs >> (c.bit_length() - 1)) == 1, causal) for c in pair_sizes]
    is_query = [((row & c) == 0) if reverse else ((row & c) != 0) for c in pair_sizes]

    for h in range(heads):
        hs = slice(h * HEAD_DIM, (h + 1) * HEAD_DIM)
        lb = lb_ref[:, hs]
        f = lb + (1.0 - lb) * jax.nn.sigmoid(f_ref[:, hs])
        f = jnp.clip(f, F_MIN, 1.0)
        kk = 1.0 - f

        g = jnp.log2(f)
        k = 1
        while k < c_len:
            if reverse:
                g = g + jnp.where(row < c_len - k, pltpu.roll(g, c_len - k, axis=0), 0.0)
            else:
                g = g + jnp.where(row >= k, pltpu.roll(g, k, axis=0), 0.0)
            k *= 2
        g_scr[:, hs] = g

        def ref_rows(block, offset):
            parts = [
                jnp.broadcast_to(g_scr[pl.ds(b * block + offset, 1), hs], (block, HEAD_DIM))
                for b in range(c_len // block)
            ]
            return parts[0] if len(parts) == 1 else jnp.concatenate(parts, axis=0)

        qh = _silu(q_ref[:, hs])
        vb = v_ref[:, hs].astype(BF16)

        gref = ref_rows(8, 4 if reverse else 3)
        xq = (qh * jnp.exp2(g - gref)).astype(BF16)
        xk = (kk * jnp.exp2(gref - g)).astype(BF16)
        att = jnp.where(mask_diag,
                        lax.dot_general(xq, xk, _NT, preferred_element_type=F32), 0.0)
        for c, m, isq in zip(pair_sizes, pair_masks, is_query):
            gref = ref_rows(2 * c, c if reverse else c - 1)
            xl = (jnp.where(isq, qh, kk) * jnp.exp2(-jnp.abs(g - gref))).astype(BF16)
            att = jnp.where(m, lax.dot_general(xl, xl, _NT, preferred_element_type=F32), att)

        g_last = g_scr[pl.ds(0 if reverse else c_len - 1, 1), hs]
        q_in = (qh * jnp.exp2(g)).astype(BF16)
        k_dec = (kk * jnp.exp2(g_last - g)).astype(BF16)
        st = st_ref[h]
        vt = vb.T
        o = lax.dot_general(jnp.concatenate([q_in, att.astype(BF16)], axis=1),
                            jnp.concatenate([st.astype(BF16), vt], axis=1),
                            _NT, preferred_element_type=F32)
        st_ref[h] = st * jnp.exp2(g_last) + jnp.dot(vt, k_dec, preferred_element_type=F32)
        if final:
            o = o + op_ref[:, hs]
            ms = jnp.mean(o * o, axis=-1, keepdims=True)
            gate = ng_ref[:, hs] * _silu(ga_ref[:, hs])
            o_ref[:, hs] = (o * lax.rsqrt(ms + RMS_EPS) * gate).astype(o_ref.dtype)
        else:
            o_ref[:, hs] = o


def _hgrn(proj, lb, batch, *, reverse, col_f, norm_g=None, o_prev=None, chunk=128):
    n = proj.shape[0]
    width = lb.shape[-1]
    heads = width // HEAD_DIM
    nc = n // batch // chunk
    final = o_prev is not None

    def tok(b, i):
        return b * nc + ((nc - 1 - i) if reverse else i)

    def col(j):
        return pl.BlockSpec((chunk, width), lambda b, i: (tok(b, i), j))

    vec = pl.BlockSpec((1, width), lambda b, i: (0, 0))
    in_specs = [col(0), col(1), col(col_f), vec]
    args = [proj, proj, proj, lb.reshape(1, width)]
    if final:
        in_specs += [col(2), vec, col(0)]
        args += [proj, norm_g.reshape(1, width), o_prev]
    return pl.pallas_call(
        functools.partial(_hgrn_kernel, reverse=reverse, final=final, chunk=chunk, heads=heads),
        grid=(batch, nc),
        in_specs=in_specs,
        out_specs=col(0),
        out_shape=jax.ShapeDtypeStruct((n, width), BF16 if final else F32),
        scratch_shapes=[
            pltpu.VMEM((heads, HEAD_DIM, HEAD_DIM), F32),
            pltpu.VMEM((chunk, width), F32),
        ],
        compiler_params=_params("arbitrary", "arbitrary"),
        name="hgrn_bwd" if reverse else "hgrn_fwd",
    )(*args)


HALF_ROWS = ROW_BLOCK // 2
KEY_ROWS = HALF_ROWS + WIN_R


def _na_table(rpb):
    hp = lax.Precision.HIGHEST
    nh = rpb.shape[0]
    qr4 = jnp.arange(HALF_ROWS)
    kr12 = jnp.arange(KEY_ROWS)
    col = jnp.arange(GRID_W)
    dr = kr12[None, :] - qr4[:, None] + (WIN_R - 1 - HALF_ROWS)
    dc = col[None, :] - col[:, None] + (WIN_C - 1)
    oh_r = (dr[..., None] == jnp.arange(2 * WIN_R - 1)).astype(F32)
    oh_c = (dc[..., None] == jnp.arange(2 * WIN_C - 1)).astype(F32)
    t1 = jnp.einsum('hab,rka->hrkb', rpb.astype(F32), oh_r, precision=hp)
    bias = jnp.einsum('hrkb,cjb->hrckj', t1, oh_c, precision=hp)

    a = jnp.arange(2)[:, None, None, None, None]
    qr = a * HALF_ROWS + qr4[None, :, None, None, None]
    kr = a * HALF_ROWS + kr12[None, None, None, :, None] - WIN_R // 2
    qc = col[None, None, :, None, None]
    kc = col[None, None, None, None, :]
    cs = jnp.clip(qc - WIN_C // 2, 0, GRID_W - WIN_C)
    col_ok = (kc >= cs) & (kc < cs + WIN_C)
    rs_mid = qr - WIN_R // 2
    tabs = []
    for rs in (jnp.maximum(rs_mid, 0), rs_mid, jnp.minimum(rs_mid, 0)):
        ok = (kr >= rs) & (kr < rs + WIN_R) & col_ok
        tabs.append(jnp.where(ok[None], bias[:, None], MASK_NEG))
    return jnp.stack(tabs).reshape(3, nh, ROW_BLOCK * GRID_W, KEY_ROWS * GRID_W)


def _na_kernel(q_ref, kp_ref, kc_ref, kn_ref, vp_ref, vc_ref, vn_ref, tab_ref, o_ref, *, scale):
    half = q_ref.shape[0] // 2
    nk = tab_ref.shape[2]
    for h in range(tab_ref.shape[0]):
        hs = slice(h * HEAD_DIM, (h + 1) * HEAD_DIM)

        def window(prev, cur, nxt):
            return jnp.concatenate([prev[half:, hs], cur[:, hs], nxt[:half, hs]],
                                   axis=0).astype(BF16)

        q = (q_ref[:, hs] * scale).astype(BF16)
        k = window(kp_ref, kc_ref, kn_ref)
        v = window(vp_ref, vc_ref, vn_ref)
        for a in range(2):
            qs = slice(a * half, (a + 1) * half)
            ks = slice(a * half, a * half + nk)
            s = lax.dot_general(q[qs], k[ks], _NT, preferred_element_type=F32) + tab_ref[h, qs, :]
            m = jnp.max(s, axis=-1, keepdims=True)
            p = jnp.exp(s - m)
            l = jnp.sum(p, axis=-1, keepdims=True)
            o = jnp.dot(p.astype(BF16), v[ks], preferred_element_type=F32) / l
            o_ref[qs, hs] = o.astype(o_ref.dtype)


def _na(proj, table, batch, col_q, heads_per_step=4):
    n = proj.shape[0]
    heads = table.shape[1]
    hp = heads_per_step
    nq = ROW_BLOCK * GRID_W
    nblk = n // batch // nq
    assert nblk >= 2 and heads % hp == 0
    cq = col_q // (hp * HEAD_DIM)
    hb = heads // hp

    def spec(which, shift):
        def index(h, b, i):
            return (b * nblk + jnp.clip(i + shift, 0, nblk - 1), cq + which * hb + h)
        return pl.BlockSpec((nq, hp * HEAD_DIM), index)

    def tab_index(h, b, i):
        return (jnp.where(i == 0, 0, jnp.where(i == nblk - 1, 2, 1)), h, 0, 0)

    return pl.pallas_call(
        functools.partial(_na_kernel, scale=HEAD_DIM ** -0.5),
        grid=(hb, batch, nblk),
        in_specs=[spec(0, 0), spec(1, -1), spec(1, 0), spec(1, 1),
                  spec(2, -1), spec(2, 0), spec(2, 1),
                  pl.BlockSpec((None, hp, nq, table.shape[3]), tab_index)],
        out_specs=pl.BlockSpec((nq, hp * HEAD_DIM), lambda h, b, i: (b * nblk + i, h)),
        out_shape=jax.ShapeDtypeStruct((n, heads * HEAD_DIM), BF16),
        compiler_params=_params("arbitrary", "arbitrary", "arbitrary"),
        name="natten",
    )(proj, proj, proj, proj, proj, proj, proj, table)


def _mix_kernel(oa_ref, ob_ref, ga_ref, gb_ref, x_ref, wb_ref, wo_ref, g_ref, b_ref,
                o_ref, ob16_ref, *, alpha):
    ya = jnp.dot(oa_ref[...], wb_ref[0], preferred_element_type=F32)
    yb = jnp.dot(ob_ref[...], wb_ref[1], preferred_element_type=F32)
    merged = jax.nn.sigmoid(ga_ref[...]) * ya + jax.nn.sigmoid(gb_ref[...]) * yb
    mix = jnp.dot(merged.astype(BF16), wo_ref[...], preferred_element_type=F32)
    y = _layer_norm(alpha * x_ref[...] + mix, g_ref[...], b_ref[...])
    o_ref[...] = y
    ob16_ref[...] = y.astype(BF16)


def _mix(oa, ob, proj, x, w_branch, w_out, ln_g, ln_b, col_gate, alpha, tm=256):
    n, d = x.shape
    wa = oa.shape[1]
    cg = col_gate // d
    const = pl.Buffered(1)
    row = pl.BlockSpec((tm, d), lambda i: (i, 0))
    return pl.pallas_call(
        functools.partial(_mix_kernel, alpha=alpha),
        grid=(n // tm,),
        in_specs=[
            pl.BlockSpec((tm, wa), lambda i: (i, 0)),
            pl.BlockSpec((tm, wa), lambda i: (i, 0)),
            pl.BlockSpec((tm, d), lambda i: (i, cg)),
            pl.BlockSpec((tm, d), lambda i: (i, cg + 1)),
            row,
            pl.BlockSpec((2, wa, d), lambda i: (0, 0, 0), pipeline_mode=const),
            pl.BlockSpec((d, d), lambda i: (0, 0), pipeline_mode=const),
            pl.BlockSpec((1, d), lambda i: (0, 0)),
            pl.BlockSpec((1, d), lambda i: (0, 0)),
        ],
        out_specs=[row, row],
        out_shape=[jax.ShapeDtypeStruct((n, d), F32), jax.ShapeDtypeStruct((n, d), BF16)],
        compiler_params=_params("arbitrary"),
        name="mix",
    )(oa, ob, proj, proj, x, w_branch, w_out, ln_g.reshape(1, d), ln_b.reshape(1, d))


def _ffn_kernel(x_ref, xb_ref, p_ref, wg_ref, wv_ref, wd_ref, wpg_ref, wpe_ref, g_ref, b_ref,
                o_ref, *, alpha, nh):
    j = pl.program_id(1)
    tc = wpg_ref.shape[1]

    @pl.when(j == 0)
    def _():
        o_ref[...] = jnp.zeros_like(o_ref)

    @pl.when(j < nh)
    def _():
        xb = xb_ref[...]
        ug = jnp.dot(xb, wg_ref[...], preferred_element_type=F32)
        uv = jnp.dot(xb, wv_ref[...], preferred_element_type=F32)
        hid = (_silu(ug) * uv).astype(BF16)
        o_ref[...] += jnp.dot(hid, wd_ref[...], preferred_element_type=F32)

    @pl.when(j >= nh)
    def _():
        cols = pl.ds(pl.multiple_of((j - nh) * tc, tc), tc)
        pg = jnp.dot(xb_ref[...], wpg_ref[...], preferred_element_type=F32)
        pe = jnp.dot(p_ref[...].astype(BF16), wpe_ref[...], preferred_element_type=F32)
        o_ref[:, cols] += jax.nn.sigmoid(pg) * pe

    @pl.when(j == pl.num_programs(1) - 1)
    def _():
        o_ref[...] = _layer_norm(alpha * x_ref[...] + o_ref[...], g_ref[...], b_ref[...])


def _ffn(x, xb, p, layer, w_up, w_down, w_pg, w_pe, ln_g, ln_b, alpha, tm=1024, th=512):
    n, d = x.shape
    hidden = w_down.shape[0]
    nh = hidden // th
    nc = d // th
    p_row0 = layer * (n // tm)

    def hid(j):
        return jnp.minimum(j, nh - 1)

    def emb(j):
        return jnp.maximum(j - nh, 0)

    return pl.pallas_call(
        functools.partial(_ffn_kernel, alpha=alpha, nh=nh),
        grid=(n // tm, nh + nc),
        in_specs=[
            pl.BlockSpec((tm, d), lambda i, j: (i, 0), pipeline_mode=pl.Buffered(1)),
            pl.BlockSpec((tm, d), lambda i, j: (i, 0)),
            pl.BlockSpec((tm, p.shape[1]), lambda i, j: (p_row0 + i, 0)),
            pl.BlockSpec((d, th), lambda i, j: (0, hid(j))),
            pl.BlockSpec((d, th), lambda i, j: (0, nh + hid(j))),
            pl.BlockSpec((th, d), lambda i, j: (hid(j), 0)),
            pl.BlockSpec((d, th), lambda i, j: (0, emb(j))),
            pl.BlockSpec((w_pe.shape[0], th), lambda i, j: (0, emb(j))),
            pl.BlockSpec((1, d), lambda i, j: (0, 0)),
            pl.BlockSpec((1, d), lambda i, j: (0, 0)),
        ],
        out_specs=pl.BlockSpec((tm, d), lambda i, j: (i, 0)),
        out_shape=jax.ShapeDtypeStruct((n, d), F32),
        compiler_params=_params("arbitrary", "arbitrary"),
        name="ffn",
    )(x, xb, p, w_up, w_up, w_down, w_pg, w_pe, ln_g.reshape(1, d), ln_b.reshape(1, d))


def kernel(x, p, w_in, b_in, lb_logits, a_norm_g, rpb, w_branch, w_out, ln1_g, ln1_b,
           w_ffn_up, w_ffn_down, w_pe, w_pg, ln2_g, ln2_b):
    batch, seq, d = x.shape
    depth = w_in.shape[0]
    n = batch * seq
    a_width = lb_logits.shape[-1]
    na_width = rpb.shape[1] * HEAD_DIM
    alpha = (2 * depth) ** 0.25
    col_f = 3
    col_q = 5 * a_width
    col_gate = 5 * a_width + 3 * na_width
    assert col_gate % d == 0 and (seq // GRID_W) % ROW_BLOCK == 0

    lb_sm = jax.nn.softmax(lb_logits.astype(F32), axis=0)
    lower = jnp.cumsum(lb_sm, axis=0) - lb_sm[0:1]

    xf = x.reshape(n, d)
    pf = p.reshape(depth * n, -1)
    for l in range(depth):
        proj = _proj(xf, w_in[l].astype(BF16), b_in[l].reshape(1, -1))
        o_fw = _hgrn(proj, lower[l, 0], batch, reverse=False, col_f=col_f)
        oa = _hgrn(proj, lower[l, 1], batch, reverse=True, col_f=col_f + 1,
                   norm_g=a_norm_g[l], o_prev=o_fw)
        ob = _na(proj, _na_table(rpb[l]), batch, col_q)
        x1, x1b = _mix(oa, ob, proj, xf, w_branch[l].astype(BF16), w_out[l].astype(BF16),
                       ln1_g[l], ln1_b[l], col_gate, alpha)
        xf = _ffn(x1, x1b, pf, l, w_ffn_up[l].astype(BF16), w_ffn_down[l].astype(BF16),
                  w_pg[l].astype(BF16), w_pe[l].astype(BF16), ln2_g[l], ln2_b[l], alpha)
    return xf.reshape(batch, seq, d)
```

```python
import functools

import jax
import jax.numpy as jnp
from jax import lax
from jax.experimental import pallas as pl
from jax.experimental.pallas import tpu as pltpu

F32 = jnp.float32
BF16 = jnp.bfloat16

HEAD_DIM = 128
GRID_W = 64
WIN_R = 8
WIN_C = 16
ROW_BLOCK = 8
F_MIN = 1e-6
NA_LEAD = 1
LEAD = 2
LN_EPS = 1e-5
RMS_EPS = 1e-6
MASK_NEG = -1e30

VMEM_LIMIT = 60 * 1024 * 1024

_NT = (((1,), (1,)), ((), ()))


def _silu(x):
    return x * jax.nn.sigmoid(x)


def _layer_norm(y, g, b):
    mu = jnp.mean(y, axis=-1, keepdims=True)
    yc = y - mu
    var = jnp.mean(yc * yc, axis=-1, keepdims=True)
    return yc * lax.rsqrt(var + LN_EPS) * g + b


def _params(*sem):
    return pltpu.CompilerParams(dimension_semantics=sem, vmem_limit_bytes=VMEM_LIMIT)


def _proj_kernel(x_ref, w_ref, b_ref, o_ref, xb_ref):
    @pl.when(pl.program_id(1) == 0)
    def _():
        xb_ref[...] = x_ref[...].astype(BF16)

    o_ref[...] = jnp.dot(xb_ref[...], w_ref[...], preferred_element_type=F32) + b_ref[...]


def _proj(x, w, b, layer, tm=1024, tn=2048):
    n, d = x.shape
    n_in = w.shape[2]
    return pl.pallas_call(
        _proj_kernel,
        grid=(n // tm, n_in // tn),
        in_specs=[
            pl.BlockSpec((tm, d), lambda i, j: (i, 0)),
            pl.BlockSpec((None, d, tn), lambda i, j: (layer, 0, j)),
            pl.BlockSpec((None, 1, tn), lambda i, j: (layer, 0, j)),
        ],
        out_specs=pl.BlockSpec((tm, tn), lambda i, j: (i, j)),
        out_shape=jax.ShapeDtypeStruct((n, n_in), F32),
        scratch_shapes=[pltpu.VMEM((tm, d), BF16)],
        compiler_params=_params("arbitrary", "arbitrary"),
        name="proj",
    )(x, w, b)


def _hgrn_kernel(*refs, reverse, final, chunk, heads):
    if final:
        q_ref, v_ref, f_ref, lb_ref, ga_ref, ng_ref, op_ref, o_ref, st_ref, g_scr = refs
    else:
        q_ref, v_ref, f_ref, lb_ref, o_ref, st_ref, g_scr = refs
    c_len = chunk

    @pl.when(pl.program_id(1) == 0)
    def _():
        st_ref[...] = jnp.zeros_like(st_ref)

    row = lax.broadcasted_iota(jnp.int32, (c_len, 1), 0)
    t_idx = lax.broadcasted_iota(jnp.int32, (c_len, c_len), 0)
    s_idx = lax.broadcasted_iota(jnp.int32, (c_len, c_len), 1)
    causal = (s_idx >= t_idx) if reverse else (s_idx <= t_idx)
    txs = t_idx ^ s_idx
    mask_diag = jnp.logical_and((txs >> 3) == 0, causal)
    pair_sizes = [c_len >> i for i in range(1, c_len.bit_length() - 3)]
    pair_masks = [jnp.logical_and((txs >> (c.bit_length() - 1)) == 1, causal) for c in pair_sizes]
    is_query = [((row & c) == 0) if reverse else ((row & c) != 0) for c in pair_sizes]

    def scores(h):
        hs = slice(h * HEAD_DIM, (h + 1) * HEAD_DIM)
        lb = lb_ref[:, hs]
        f = lb + (1.0 - lb) * jax.nn.sigmoid(f_ref[:, hs])
        f = jnp.clip(f, F_MIN, 1.0)
        kk = 1.0 - f

        g = jnp.log2(f)
        k = 1
        while k < c_len:
            if reverse:
                g = g + jnp.where(row < c_len - k, pltpu.roll(g, c_len - k, axis=0), 0.0)
            else:
                g = g + jnp.where(row >= k, pltpu.roll(g, k, axis=0), 0.0)
            k *= 2
        g_scr[:, hs] = g

        def ref_rows(block, offset):
            parts = [
                jnp.broadcast_to(g_scr[pl.ds(b * block + offset, 1), hs], (block, HEAD_DIM))
                for b in range(c_len // block)
            ]
            return parts[0] if len(parts) == 1 else jnp.concatenate(parts, axis=0)

        qh = _silu(q_ref[:, hs])
        gref = ref_rows(8, 4 if reverse else 3)
        xq = (qh * jnp.exp2(g - gref)).astype(BF16)
        xk = (kk * jnp.exp2(gref - g)).astype(BF16)
        att = jnp.where(mask_diag,
                        lax.dot_general(xq, xk, _NT, preferred_element_type=F32), 0.0)
        for c, m, isq in zip(pair_sizes, pair_masks, is_query):
            gref = ref_rows(2 * c, c if reverse else c - 1)
            xl = (jnp.where(isq, qh, kk) * jnp.exp2(-jnp.abs(g - gref))).astype(BF16)
            att = jnp.where(m, lax.dot_general(xl, xl, _NT, preferred_element_type=F32), att)

        g_last = g_scr[pl.ds(0 if reverse else c_len - 1, 1), hs]
        q_in = (qh * jnp.exp2(g)).astype(BF16)
        k_dec = (kk * jnp.exp2(g_last - g)).astype(BF16)
        return att.astype(BF16), q_in, k_dec, jnp.exp2(g_last)

    def outputs(h, att, q_in, k_dec, d_last):
        hs = slice(h * HEAD_DIM, (h + 1) * HEAD_DIM)
        st = st_ref[h]
        vt = v_ref[:, hs].astype(BF16).T
        o = lax.dot_general(jnp.concatenate([q_in, att], axis=1),
                            jnp.concatenate([st.astype(BF16), vt], axis=1),
                            _NT, preferred_element_type=F32)
        st_ref[h] = st * d_last + jnp.dot(vt, k_dec, preferred_element_type=F32)
        if final:
            o = o + op_ref[:, hs]
            ms = jnp.mean(o * o, axis=-1, keepdims=True)
            gate = ng_ref[:, hs] * _silu(ga_ref[:, hs])
            o_ref[:, hs] = (o * lax.rsqrt(ms + RMS_EPS) * gate).astype(o_ref.dtype)
        else:
            o_ref[:, hs] = o

    pending = []
    for h in range(heads):
        pending.append((h,) + scores(h))
        if len(pending) > LEAD:
            outputs(*pending.pop(0))
    for item in pending:
        outputs(*item)


def _hgrn(proj, lb, batch, *, reverse, col_f, norm_g=None, o_prev=None, chunk=128):
    n = proj.shape[0]
    width = lb.shape[-1]
    heads = width // HEAD_DIM
    nc = n // batch // chunk
    final = o_prev is not None

    def tok(b, i):
        return b * nc + ((nc - 1 - i) if reverse else i)

    def col(j):
        return pl.BlockSpec((chunk, width), lambda b, i: (tok(b, i), j))

    vec = pl.BlockSpec((1, width), lambda b, i: (0, 0))
    in_specs = [col(0), col(1), col(col_f), vec]
    args = [proj, proj, proj, lb.reshape(1, width)]
    if final:
        in_specs += [col(2), vec, col(0)]
        args += [proj, norm_g.reshape(1, width), o_prev]
    return pl.pallas_call(
        functools.partial(_hgrn_kernel, reverse=reverse, final=final, chunk=chunk, heads=heads),
        grid=(batch, nc),
        in_specs=in_specs,
        out_specs=col(0),
        out_shape=jax.ShapeDtypeStruct((n, width), BF16 if final else F32),
        scratch_shapes=[
            pltpu.VMEM((heads, HEAD_DIM, HEAD_DIM), F32),
            pltpu.VMEM((chunk, width), F32),
        ],
        compiler_params=_params("arbitrary", "arbitrary"),
        name="hgrn_bwd" if reverse else "hgrn_fwd",
    )(*args)


HALF_ROWS = ROW_BLOCK // 2
KEY_ROWS = HALF_ROWS + WIN_R


def _na_table(rpb):
    hp = lax.Precision.HIGHEST
    nh = rpb.shape[0]
    qr4 = jnp.arange(HALF_ROWS)
    kr12 = jnp.arange(KEY_ROWS)
    col = jnp.arange(GRID_W)
    dr = kr12[None, :] - qr4[:, None] + (WIN_R - 1 - HALF_ROWS)
    dc = col[None, :] - col[:, None] + (WIN_C - 1)
    oh_r = (dr[..., None] == jnp.arange(2 * WIN_R - 1)).astype(F32)
    oh_c = (dc[..., None] == jnp.arange(2 * WIN_C - 1)).astype(F32)
    t1 = jnp.einsum('hab,rka->hrkb', rpb.astype(F32), oh_r, precision=hp)
    bias = jnp.einsum('hrkb,cjb->hrckj', t1, oh_c, precision=hp)

    a = jnp.arange(2)[:, None, None, None, None]
    qr = a * HALF_ROWS + qr4[None, :, None, None, None]
    kr = a * HALF_ROWS + kr12[None, None, None, :, None] - WIN_R // 2
    qc = col[None, None, :, None, None]
    kc = col[None, None, None, None, :]
    cs = jnp.clip(qc - WIN_C // 2, 0, GRID_W - WIN_C)
    col_ok = (kc >= cs) & (kc < cs + WIN_C)
    rs_mid = qr - WIN_R // 2
    tabs = []
    for rs in (jnp.maximum(rs_mid, 0), rs_mid, jnp.minimum(rs_mid, 0)):
        ok = (kr >= rs) & (kr < rs + WIN_R) & col_ok
        tabs.append(jnp.where(ok[None], bias[:, None], MASK_NEG))
    return jnp.stack(tabs).reshape(3, nh, ROW_BLOCK * GRID_W, KEY_ROWS * GRID_W)


def _na_kernel(q_ref, kp_ref, kc_ref, kn_ref, vp_ref, vc_ref, vn_ref, tab_ref, o_ref, *, scale):
    half = q_ref.shape[0] // 2
    nk = tab_ref.shape[2]

    def logits(h, a, q, k):
        qs = slice(a * half, (a + 1) * half)
        return lax.dot_general(q[qs], k[a * half:a * half + nk], _NT,
                               preferred_element_type=F32) + tab_ref[h, qs, :]

    def attend(h, a, s, v):
        hs = slice(h * HEAD_DIM, (h + 1) * HEAD_DIM)
        m = jnp.max(s, axis=-1, keepdims=True)
        p = jnp.exp(s - m)
        l = jnp.sum(p, axis=-1, keepdims=True)
        o = jnp.dot(p.astype(BF16), v[a * half:a * half + nk], preferred_element_type=F32) / l
        o_ref[a * half:(a + 1) * half, hs] = o.astype(o_ref.dtype)

    pending = []
    for h in range(tab_ref.shape[0]):
        hs = slice(h * HEAD_DIM, (h + 1) * HEAD_DIM)

        def window(prev, cur, nxt):
            return jnp.concatenate([prev[half:, hs], cur[:, hs], nxt[:half, hs]],
                                   axis=0).astype(BF16)

        q = (q_ref[:, hs] * scale).astype(BF16)
        k = window(kp_ref, kc_ref, kn_ref)
        v = window(vp_ref, vc_ref, vn_ref)
        for a in range(2):
            pending.append((h, a, logits(h, a, q, k), v))
            if len(pending) > NA_LEAD:
                attend(*pending.pop(0))
    for item in pending:
        attend(*item)


def _na(proj, table, batch, col_q, heads_per_step=4):
    n = proj.shape[0]
    heads = table.shape[1]
    hp = heads_per_step
    nq = ROW_BLOCK * GRID_W
    nblk = n // batch // nq
    assert nblk >= 2 and heads % hp == 0
    cq = col_q // (hp * HEAD_DIM)
    hb = heads // hp

    def spec(which, shift):
        def index(h, b, i):
            return (b * nblk + jnp.clip(i + shift, 0, nblk - 1), cq + which * hb + h)
        return pl.BlockSpec((nq, hp * HEAD_DIM), index)

    def tab_index(h, b, i):
        return (jnp.where(i == 0, 0, jnp.where(i == nblk - 1, 2, 1)), h, 0, 0)

    return pl.pallas_call(
        functools.partial(_na_kernel, scale=HEAD_DIM ** -0.5),
        grid=(hb, batch, nblk),
        in_specs=[spec(0, 0), spec(1, -1), spec(1, 0), spec(1, 1),
                  spec(2, -1), spec(2, 0), spec(2, 1),
                  pl.BlockSpec((None, hp, nq, table.shape[3]), tab_index)],
        out_specs=pl.BlockSpec((nq, hp * HEAD_DIM), lambda h, b, i: (b * nblk + i, h)),
        out_shape=jax.ShapeDtypeStruct((n, heads * HEAD_DIM), BF16),
        compiler_params=_params("arbitrary", "arbitrary", "arbitrary"),
        name="natten",
    )(proj, proj, proj, proj, proj, proj, proj, table)


def _mix_kernel(oa_ref, ob_ref, ga_ref, gb_ref, x_ref, wb_ref, wo_ref, g_ref, b_ref,
                o_ref, ob16_ref, *, alpha):
    ya = jnp.dot(oa_ref[...], wb_ref[0], preferred_element_type=F32)
    yb = jnp.dot(ob_ref[...], wb_ref[1], preferred_element_type=F32)
    merged = jax.nn.sigmoid(ga_ref[...]) * ya + jax.nn.sigmoid(gb_ref[...]) * yb
    mix = jnp.dot(merged.astype(BF16), wo_ref[...], preferred_element_type=F32)
    y = _layer_norm(alpha * x_ref[...] + mix, g_ref[...], b_ref[...])
    o_ref[...] = y
    ob16_ref[...] = y.astype(BF16)


def _mix(oa, ob, proj, x, w_branch, w_out, layer, ln_g, ln_b, col_gate, alpha, tm=256):
    n, d = x.shape
    wa = oa.shape[1]
    cg = col_gate // d
    const = pl.Buffered(1)
    row = pl.BlockSpec((tm, d), lambda i: (i, 0))
    return pl.pallas_call(
        functools.partial(_mix_kernel, alpha=alpha),
        grid=(n // tm,),
        in_specs=[
            pl.BlockSpec((tm, wa), lambda i: (i, 0)),
            pl.BlockSpec((tm, wa), lambda i: (i, 0)),
            pl.BlockSpec((tm, d), lambda i: (i, cg)),
            pl.BlockSpec((tm, d), lambda i: (i, cg + 1)),
            row,
            pl.BlockSpec((None, 2, wa, d), lambda i: (layer, 0, 0, 0), pipeline_mode=const),
            pl.BlockSpec((None, d, d), lambda i: (layer, 0, 0), pipeline_mode=const),
            pl.BlockSpec((1, d), lambda i: (0, 0)),
            pl.BlockSpec((1, d), lambda i: (0, 0)),
        ],
        out_specs=[row, row],
        out_shape=[jax.ShapeDtypeStruct((n, d), F32), jax.ShapeDtypeStruct((n, d), BF16)],
        compiler_params=_params("arbitrary"),
        name="mix",
    )(oa, ob, proj, proj, x, w_branch, w_out, ln_g.reshape(1, d), ln_b.reshape(1, d))


def _ffn_kernel(x_ref, xb_ref, p_ref, wg_ref, wv_ref, wd_ref, wpg_ref, wpe_ref, g_ref, b_ref,
                o_ref, *, alpha, nh):
    j = pl.program_id(1)
    tc = wpg_ref.shape[1]

    @pl.when(j == 0)
    def _():
        o_ref[...] = jnp.zeros_like(o_ref)

    @pl.when(j < nh)
    def _():
        xb = xb_ref[...]
        ug = jnp.dot(xb, wg_ref[...], preferred_element_type=F32)
        uv = jnp.dot(xb, wv_ref[...], preferred_element_type=F32)
        hid = (_silu(ug) * uv).astype(BF16)
        o_ref[...] += jnp.dot(hid, wd_ref[...], preferred_element_type=F32)

    @pl.when(j >= nh)
    def _():
        cols = pl.ds(pl.multiple_of((j - nh) * tc, tc), tc)
        pg = jnp.dot(xb_ref[...], wpg_ref[...], preferred_element_type=F32)
        pe = jnp.dot(p_ref[...].astype(BF16), wpe_ref[...], preferred_element_type=F32)
        o_ref[:, cols] += jax.nn.sigmoid(pg) * pe

    @pl.when(j == pl.num_programs(1) - 1)
    def _():
        o_ref[...] = _layer_norm(alpha * x_ref[...] + o_ref[...], g_ref[...], b_ref[...])


def _ffn(x, xb, p, layer, w_up, w_down, w_pg, w_pe, ln_g, ln_b, alpha, tm=1024, th=512):
    n, d = x.shape
    hidden = w_down.shape[1]
    nh = hidden // th
    nc = d // th
    p_row0 = layer * (n // tm)

    def hid(j):
        return jnp.minimum(j, nh - 1)

    def emb(j):
        return jnp.maximum(j - nh, 0)

    return pl.pallas_call(
        functools.partial(_ffn_kernel, alpha=alpha, nh=nh),
        grid=(n // tm, nh + nc),
        in_specs=[
            pl.BlockSpec((tm, d), lambda i, j: (i, 0), pipeline_mode=pl.Buffered(1)),
            pl.BlockSpec((tm, d), lambda i, j: (i, 0)),
            pl.BlockSpec((tm, p.shape[1]), lambda i, j: (p_row0 + i, 0)),
            pl.BlockSpec((None, d, th), lambda i, j: (layer, 0, hid(j))),
            pl.BlockSpec((None, d, th), lambda i, j: (layer, 0, nh + hid(j))),
            pl.BlockSpec((None, th, d), lambda i, j: (layer, hid(j), 0)),
            pl.BlockSpec((None, d, th), lambda i, j: (layer, 0, emb(j))),
            pl.BlockSpec((None, w_pe.shape[1], th), lambda i, j: (layer, 0, emb(j))),
            pl.BlockSpec((1, d), lambda i, j: (0, 0)),
            pl.BlockSpec((1, d), lambda i, j: (0, 0)),
        ],
        out_specs=pl.BlockSpec((tm, d), lambda i, j: (i, 0)),
        out_shape=jax.ShapeDtypeStruct((n, d), F32),
        compiler_params=_params("arbitrary", "arbitrary"),
        name="ffn",
    )(x, xb, p, w_up, w_up, w_down, w_pg, w_pe, ln_g.reshape(1, d), ln_b.reshape(1, d))


def kernel(x, p, w_in, b_in, lb_logits, a_norm_g, rpb, w_branch, w_out, ln1_g, ln1_b,
           w_ffn_up, w_ffn_down, w_pe, w_pg, ln2_g, ln2_b):
    batch, seq, d = x.shape
    depth = w_in.shape[0]
    n = batch * seq
    a_width = lb_logits.shape[-1]
    na_width = rpb.shape[1] * HEAD_DIM
    alpha = (2 * depth) ** 0.25
    col_f = 3
    col_q = 5 * a_width
    col_gate = 5 * a_width + 3 * na_width
    assert col_gate % d == 0 and (seq // GRID_W) % ROW_BLOCK == 0

    lb_sm = jax.nn.softmax(lb_logits.astype(F32), axis=0)
    lower = jnp.cumsum(lb_sm, axis=0) - lb_sm[0:1]

    w_in, w_branch, w_out, w_ffn_up, w_ffn_down, w_pg, w_pe = (
        w.astype(BF16) for w in (w_in, w_branch, w_out, w_ffn_up, w_ffn_down, w_pg, w_pe))
    b_in = b_in.reshape(depth, 1, -1)

    xf = x.reshape(n, d)
    pf = p.reshape(depth * n, -1)
    for l in range(depth):
        proj = _proj(xf, w_in, b_in, l)
        o_fw = _hgrn(proj, lower[l, 0], batch, reverse=False, col_f=col_f)
        oa = _hgrn(proj, lower[l, 1], batch, reverse=True, col_f=col_f + 1,
                   norm_g=a_norm_g[l], o_prev=o_fw)
        ob = _na(proj, _na_table(rpb[l]), batch, col_q)
        x1, x1b = _mix(oa, ob, proj, xf, w_branch, w_out, l, ln1_g[l], ln1_b[l], col_gate, alpha)
        xf = _ffn(x1, x1b, pf, l, w_ffn_up, w_ffn_down, w_pg, w_pe, ln2_g[l], ln2_b[l], alpha)
    return xf.reshape(batch, seq, d)
```

```python
import functools

import jax
import jax.numpy as jnp
from jax import lax
from jax.experimental import pallas as pl
from jax.experimental.pallas import tpu as pltpu

F32 = jnp.float32
BF16 = jnp.bfloat16

HEAD_DIM = 128
GRID_W = 64
WIN_R = 8
WIN_C = 16
ROW_BLOCK = 8
F_MIN = 1e-6
NA_LEAD = 1
LEAD = 4
FFN_TILE = 512
LN_EPS = 1e-5
RMS_EPS = 1e-6
MASK_NEG = -1e30

VMEM_LIMIT = 60 * 1024 * 1024

_NT = (((1,), (1,)), ((), ()))


def _silu(x):
    return x * jax.nn.sigmoid(x)


def _layer_norm(y, g, b):
    mu = jnp.mean(y, axis=-1, keepdims=True)
    yc = y - mu
    var = jnp.mean(yc * yc, axis=-1, keepdims=True)
    return yc * lax.rsqrt(var + LN_EPS) * g + b


def _params(*sem):
    return pltpu.CompilerParams(dimension_semantics=sem, vmem_limit_bytes=VMEM_LIMIT)


def _proj_kernel(x_ref, w_ref, b_ref, o_ref, xb_ref):
    @pl.when(pl.program_id(1) == 0)
    def _():
        xb_ref[...] = x_ref[...].astype(BF16)

    o_ref[...] = jnp.dot(xb_ref[...], w_ref[...], preferred_element_type=F32) + b_ref[...]


def _proj(x, w, b, layer, tm=1024, tn=2048):
    n, d = x.shape
    n_in = w.shape[2]
    return pl.pallas_call(
        _proj_kernel,
        grid=(n // tm, n_in // tn),
        in_specs=[
            pl.BlockSpec((tm, d), lambda i, j: (i, 0)),
            pl.BlockSpec((None, d, tn), lambda i, j: (layer, 0, j)),
            pl.BlockSpec((None, 1, tn), lambda i, j: (layer, 0, j)),
        ],
        out_specs=pl.BlockSpec((tm, tn), lambda i, j: (i, j)),
        out_shape=jax.ShapeDtypeStruct((n, n_in), F32),
        scratch_shapes=[pltpu.VMEM((tm, d), BF16)],
        compiler_params=_params("arbitrary", "arbitrary"),
        name="proj",
    )(x, w, b)


def _hgrn_kernel(*refs, reverse, final, chunk, heads):
    if final:
        q_ref, v_ref, f_ref, lb_ref, ga_ref, ng_ref, op_ref, o_ref, st_ref, g_scr = refs
    else:
        q_ref, v_ref, f_ref, lb_ref, o_ref, st_ref, g_scr = refs
    c_len = chunk
    batch = q_ref.shape[0]

    @pl.when(pl.program_id(0) == 0)
    def _():
        st_ref[...] = jnp.zeros_like(st_ref)

    row = lax.broadcasted_iota(jnp.int32, (c_len, 1), 0)
    t_idx = lax.broadcasted_iota(jnp.int32, (c_len, c_len), 0)
    s_idx = lax.broadcasted_iota(jnp.int32, (c_len, c_len), 1)
    causal = (s_idx >= t_idx) if reverse else (s_idx <= t_idx)
    txs = t_idx ^ s_idx
    mask_diag = jnp.logical_and((txs >> 3) == 0, causal)
    pair_sizes = [c_len >> i for i in range(1, c_len.bit_length() - 3)]
    pair_masks = [jnp.logical_and((txs >> (c.bit_length() - 1)) == 1, causal) for c in pair_sizes]
    is_query = [((row & c) == 0) if reverse else ((row & c) != 0) for c in pair_sizes]

    def scores(h, b):
        hs = slice(h * HEAD_DIM, (h + 1) * HEAD_DIM)
        lb = lb_ref[:, hs]
        f = lb + (1.0 - lb) * jax.nn.sigmoid(f_ref[b, :, hs])
        f = jnp.clip(f, F_MIN, 1.0)
        kk = 1.0 - f

        g = jnp.log2(f)
        k = 1
        while k < c_len:
            if reverse:
                g = g + jnp.where(row < c_len - k, pltpu.roll(g, c_len - k, axis=0), 0.0)
            else:
                g = g + jnp.where(row >= k, pltpu.roll(g, k, axis=0), 0.0)
            k *= 2
        g_scr[b, :, hs] = g

        def ref_rows(block, offset):
            parts = [
                jnp.broadcast_to(g_scr[b, pl.ds(k * block + offset, 1), hs], (block, HEAD_DIM))
                for k in range(c_len // block)
            ]
            return parts[0] if len(parts) == 1 else jnp.concatenate(parts, axis=0)

        qh = _silu(q_ref[b, :, hs])
        gref = ref_rows(8, 4 if reverse else 3)
        xq = (qh * jnp.exp2(g - gref)).astype(BF16)
        xk = (kk * jnp.exp2(gref - g)).astype(BF16)
        att = jnp.where(mask_diag,
                        lax.dot_general(xq, xk, _NT, preferred_element_type=F32), 0.0)
        for c, m, isq in zip(pair_sizes, pair_masks, is_query):
            gref = ref_rows(2 * c, c if reverse else c - 1)
            xl = (jnp.where(isq, qh, kk) * jnp.exp2(-jnp.abs(g - gref))).astype(BF16)
            att = jnp.where(m, lax.dot_general(xl, xl, _NT, preferred_element_type=F32), att)

        g_last = g_scr[b, pl.ds(0 if reverse else c_len - 1, 1), hs]
        q_in = (qh * jnp.exp2(g)).astype(BF16)
        k_dec = (kk * jnp.exp2(g_last - g)).astype(BF16)
        return att.astype(BF16), q_in, k_dec, jnp.exp2(g_last)

    def outputs(h, b, att, q_in, k_dec, d_last):
        hs = slice(h * HEAD_DIM, (h + 1) * HEAD_DIM)
        st = st_ref[b * heads + h]
        vt = v_ref[b, :, hs].astype(BF16).T
        o = lax.dot_general(jnp.concatenate([q_in, att], axis=1),
                            jnp.concatenate([st.astype(BF16), vt], axis=1),
                            _NT, preferred_element_type=F32)
        st_ref[b * heads + h] = st * d_last + jnp.dot(vt, k_dec, preferred_element_type=F32)
        if final:
            o = o + op_ref[b, :, hs]
            ms = jnp.mean(o * o, axis=-1, keepdims=True)
            gate = ng_ref[:, hs] * _silu(ga_ref[b, :, hs])
            o_ref[b, :, hs] = (o * lax.rsqrt(ms + RMS_EPS) * gate).astype(o_ref.dtype)
        else:
            o_ref[b, :, hs] = o

    pending = []
    for h in range(heads):
        for b in range(batch):
            pending.append((h, b) + scores(h, b))
            if len(pending) > LEAD:
                outputs(*pending.pop(0))
    for item in pending:
        outputs(*item)


def _hgrn(proj, lb, batch, *, reverse, col_f, norm_g=None, o_prev=None, chunk=128):
    n, n_in = proj.shape
    seq = n // batch
    width = lb.shape[-1]
    heads = width // HEAD_DIM
    nc = seq // chunk
    final = o_prev is not None
    proj3 = proj.reshape(batch, seq, n_in)

    def col(j):
        return pl.BlockSpec((batch, chunk, width),
                            lambda i: (0, (nc - 1 - i) if reverse else i, j))

    vec = pl.BlockSpec((1, width), lambda i: (0, 0))
    in_specs = [col(0), col(1), col(col_f), vec]
    args = [proj3, proj3, proj3, lb.reshape(1, width)]
    if final:
        in_specs += [col(2), vec, col(0)]
        args += [proj3, norm_g.reshape(1, width), o_prev.reshape(batch, seq, width)]
    out = pl.pallas_call(
        functools.partial(_hgrn_kernel, reverse=reverse, final=final, chunk=chunk, heads=heads),
        grid=(nc,),
        in_specs=in_specs,
        out_specs=col(0),
        out_shape=jax.ShapeDtypeStruct((batch, seq, width), BF16 if final else F32),
        scratch_shapes=[
            pltpu.VMEM((batch * heads, HEAD_DIM, HEAD_DIM), F32),
            pltpu.VMEM((batch, chunk, width), F32),
        ],
        compiler_params=_params("arbitrary"),
        name="hgrn_bwd" if reverse else "hgrn_fwd",
    )(*args)
    return out.reshape(n, width)


HALF_ROWS = ROW_BLOCK // 2
KEY_ROWS = HALF_ROWS + WIN_R


def _na_table(rpb):
    hp = lax.Precision.HIGHEST
    nh = rpb.shape[0]
    qr4 = jnp.arange(HALF_ROWS)
    kr12 = jnp.arange(KEY_ROWS)
    col = jnp.arange(GRID_W)
    dr = kr12[None, :] - qr4[:, None] + (WIN_R - 1 - HALF_ROWS)
    dc = col[None, :] - col[:, None] + (WIN_C - 1)
    oh_r = (dr[..., None] == jnp.arange(2 * WIN_R - 1)).astype(F32)
    oh_c = (dc[..., None] == jnp.arange(2 * WIN_C - 1)).astype(F32)
    t1 = jnp.einsum('hab,rka->hrkb', rpb.astype(F32), oh_r, precision=hp)
    bias = jnp.einsum('hrkb,cjb->hrckj', t1, oh_c, precision=hp)

    a = jnp.arange(2)[:, None, None, None, None]
    qr = a * HALF_ROWS + qr4[None, :, None, None, None]
    kr = a * HALF_ROWS + kr12[None, None, None, :, None] - WIN_R // 2
    qc = col[None, None, :, None, None]
    kc = col[None, None, None, None, :]
    cs = jnp.clip(qc - WIN_C // 2, 0, GRID_W - WIN_C)
    col_ok = (kc >= cs) & (kc < cs + WIN_C)
    rs_mid = qr - WIN_R // 2
    tabs = []
    for rs in (jnp.maximum(rs_mid, 0), rs_mid, jnp.minimum(rs_mid, 0)):
        ok = (kr >= rs) & (kr < rs + WIN_R) & col_ok
        tabs.append(jnp.where(ok[None], bias[:, None], MASK_NEG))
    return jnp.stack(tabs).reshape(3, nh, ROW_BLOCK * GRID_W, KEY_ROWS * GRID_W)


def _na_kernel(q_ref, kp_ref, kc_ref, kn_ref, vp_ref, vc_ref, vn_ref, tab_ref, o_ref, *, scale):
    half = q_ref.shape[0] // 2
    nk = tab_ref.shape[2]

    def logits(h, a, q, k):
        qs = slice(a * half, (a + 1) * half)
        return lax.dot_general(q[qs], k[a * half:a * half + nk], _NT,
                               preferred_element_type=F32) + tab_ref[h, qs, :]

    def attend(h, a, s, v):
        hs = slice(h * HEAD_DIM, (h + 1) * HEAD_DIM)
        m = jnp.max(s, axis=-1, keepdims=True)
        p = jnp.exp(s - m)
        l = jnp.sum(p, axis=-1, keepdims=True)
        o = jnp.dot(p.astype(BF16), v[a * half:a * half + nk], preferred_element_type=F32) / l
        o_ref[a * half:(a + 1) * half, hs] = o.astype(o_ref.dtype)

    pending = []
    for h in range(tab_ref.shape[0]):
        hs = slice(h * HEAD_DIM, (h + 1) * HEAD_DIM)

        def window(prev, cur, nxt):
            return jnp.concatenate([prev[half:, hs], cur[:, hs], nxt[:half, hs]],
                                   axis=0).astype(BF16)

        q = (q_ref[:, hs] * scale).astype(BF16)
        k = window(kp_ref, kc_ref, kn_ref)
        v = window(vp_ref, vc_ref, vn_ref)
        for a in range(2):
            pending.append((h, a, logits(h, a, q, k), v))
            if len(pending) > NA_LEAD:
                attend(*pending.pop(0))
    for item in pending:
        attend(*item)


def _na(proj, table, batch, col_q, heads_per_step=4):
    n = proj.shape[0]
    heads = table.shape[1]
    hp = heads_per_step
    nq = ROW_BLOCK * GRID_W
    nblk = n // batch // nq
    assert nblk >= 2 and heads % hp == 0
    cq = col_q // (hp * HEAD_DIM)
    hb = heads // hp

    def spec(which, shift):
        def index(h, b, i):
            return (b * nblk + jnp.clip(i + shift, 0, nblk - 1), cq + which * hb + h)
        return pl.BlockSpec((nq, hp * HEAD_DIM), index)

    def tab_index(h, b, i):
        return (jnp.where(i == 0, 0, jnp.where(i == nblk - 1, 2, 1)), h, 0, 0)

    return pl.pallas_call(
        functools.partial(_na_kernel, scale=HEAD_DIM ** -0.5),
        grid=(hb, batch, nblk),
        in_specs=[spec(0, 0), spec(1, -1), spec(1, 0), spec(1, 1),
                  spec(2, -1), spec(2, 0), spec(2, 1),
                  pl.BlockSpec((None, hp, nq, table.shape[3]), tab_index)],
        out_specs=pl.BlockSpec((nq, hp * HEAD_DIM), lambda h, b, i: (b * nblk + i, h)),
        out_shape=jax.ShapeDtypeStruct((n, heads * HEAD_DIM), BF16),
        compiler_params=_params("arbitrary", "arbitrary", "arbitrary"),
        name="natten",
    )(proj, proj, proj, proj, proj, proj, proj, table)


def _mix_kernel(oa_ref, ob_ref, ga_ref, gb_ref, x_ref, wb_ref, wo_ref, g_ref, b_ref,
                o_ref, ob16_ref, *, alpha):
    ya = jnp.dot(oa_ref[...], wb_ref[0], preferred_element_type=F32)
    yb = jnp.dot(ob_ref[...], wb_ref[1], preferred_element_type=F32)
    merged = jax.nn.sigmoid(ga_ref[...]) * ya + jax.nn.sigmoid(gb_ref[...]) * yb
    mix = jnp.dot(merged.astype(BF16), wo_ref[...], preferred_element_type=F32)
    y = _layer_norm(alpha * x_ref[...] + mix, g_ref[...], b_ref[...])
    o_ref[...] = y
    ob16_ref[...] = y.astype(BF16)


def _mix(oa, ob, proj, x, w_branch, w_out, layer, ln_g, ln_b, col_gate, alpha, tm=256):
    n, d = x.shape
    wa = oa.shape[1]
    cg = col_gate // d
    const = pl.Buffered(1)
    row = pl.BlockSpec((tm, d), lambda i: (i, 0))
    return pl.pallas_call(
        functools.partial(_mix_kernel, alpha=alpha),
        grid=(n // tm,),
        in_specs=[
            pl.BlockSpec((tm, wa), lambda i: (i, 0)),
            pl.BlockSpec((tm, wa), lambda i: (i, 0)),
            pl.BlockSpec((tm, d), lambda i: (i, cg)),
            pl.BlockSpec((tm, d), lambda i: (i, cg + 1)),
            row,
            pl.BlockSpec((None, 2, wa, d), lambda i: (layer, 0, 0, 0), pipeline_mode=const),
            pl.BlockSpec((None, d, d), lambda i: (layer, 0, 0), pipeline_mode=const),
            pl.BlockSpec((1, d), lambda i: (0, 0)),
            pl.BlockSpec((1, d), lambda i: (0, 0)),
        ],
        out_specs=[row, row],
        out_shape=[jax.ShapeDtypeStruct((n, d), F32), jax.ShapeDtypeStruct((n, d), BF16)],
        compiler_params=_params("arbitrary"),
        name="mix",
    )(oa, ob, proj, proj, x, w_branch, w_out, ln_g.reshape(1, d), ln_b.reshape(1, d))


def _ffn_kernel(x_ref, xb_ref, p_ref, wu_ref, wd_ref, wpg_ref, wpe_ref, g_ref, b_ref,
                o_ref, *, alpha, nh):
    j = pl.program_id(1)
    tc = wpg_ref.shape[1]

    @pl.when(j == 0)
    def _():
        o_ref[...] = jnp.zeros_like(o_ref)

    @pl.when(j < nh)
    def _():
        th = wd_ref.shape[0]
        up = jnp.dot(xb_ref[...], wu_ref[...], preferred_element_type=F32)
        hid = (_silu(up[:, :th]) * up[:, th:]).astype(BF16)
        o_ref[...] += jnp.dot(hid, wd_ref[...], preferred_element_type=F32)

    @pl.when(j >= nh)
    def _():
        cols = pl.ds(pl.multiple_of((j - nh) * tc, tc), tc)
        pg = jnp.dot(xb_ref[...], wpg_ref[...], preferred_element_type=F32)
        pe = jnp.dot(p_ref[...].astype(BF16), wpe_ref[...], preferred_element_type=F32)
        o_ref[:, cols] += jax.nn.sigmoid(pg) * pe

    @pl.when(j == pl.num_programs(1) - 1)
    def _():
        o_ref[...] = _layer_norm(alpha * x_ref[...] + o_ref[...], g_ref[...], b_ref[...])


def _tiles(w, th):
    depth, k, m = w.shape
    return w.reshape(depth, k, m // th, th).transpose(0, 2, 1, 3)


def _ffn(x, xb, p, layer, w_up, w_down, w_pg, w_pe, ln_g, ln_b, alpha, tm=1024):
    n, d = x.shape
    nh = w_up.shape[1]
    th = w_up.shape[3] // 2
    nc = d // th
    p_row0 = layer * (n // tm)

    def hid(j):
        return jnp.minimum(j, nh - 1)

    def emb(j):
        return jnp.maximum(j - nh, 0)

    return pl.pallas_call(
        functools.partial(_ffn_kernel, alpha=alpha, nh=nh),
        grid=(n // tm, nh + nc),
        in_specs=[
            pl.BlockSpec((tm, d), lambda i, j: (i, 0), pipeline_mode=pl.Buffered(1)),
            pl.BlockSpec((tm, d), lambda i, j: (i, 0)),
            pl.BlockSpec((tm, p.shape[1]), lambda i, j: (p_row0 + i, 0)),
            pl.BlockSpec((None, None, d, 2 * th), lambda i, j: (layer, hid(j), 0, 0)),
            pl.BlockSpec((None, th, d), lambda i, j: (layer, hid(j), 0)),
            pl.BlockSpec((None, None, d, th), lambda i, j: (layer, emb(j), 0, 0)),
            pl.BlockSpec((None, w_pe.shape[1], th), lambda i, j: (layer, 0, emb(j))),
            pl.BlockSpec((1, d), lambda i, j: (0, 0)),
            pl.BlockSpec((1, d), lambda i, j: (0, 0)),
        ],
        out_specs=pl.BlockSpec((tm, d), lambda i, j: (i, 0)),
        out_shape=jax.ShapeDtypeStruct((n, d), F32),
        compiler_params=_params("arbitrary", "arbitrary"),
        name="ffn",
    )(x, xb, p, w_up, w_down, w_pg, w_pe, ln_g.reshape(1, d), ln_b.reshape(1, d))


def kernel(x, p, w_in, b_in, lb_logits, a_norm_g, rpb, w_branch, w_out, ln1_g, ln1_b,
           w_ffn_up, w_ffn_down, w_pe, w_pg, ln2_g, ln2_b):
    batch, seq, d = x.shape
    depth = w_in.shape[0]
    n = batch * seq
    a_width = lb_logits.shape[-1]
    na_width = rpb.shape[1] * HEAD_DIM
    alpha = (2 * depth) ** 0.25
    col_f = 3
    col_q = 5 * a_width
    col_gate = 5 * a_width + 3 * na_width
    assert col_gate % d == 0 and (seq // GRID_W) % ROW_BLOCK == 0

    lb_sm = jax.nn.softmax(lb_logits.astype(F32), axis=0)
    lower = jnp.cumsum(lb_sm, axis=0) - lb_sm[0:1]

    w_in, w_branch, w_out, w_ffn_up, w_ffn_down, w_pg, w_pe = (
        w.astype(BF16) for w in (w_in, w_branch, w_out, w_ffn_up, w_ffn_down, w_pg, w_pe))
    b_in = b_in.reshape(depth, 1, -1)
    hidden = w_ffn_down.shape[1]
    w_ffn_up = jnp.concatenate([_tiles(w_ffn_up[:, :, :hidden], FFN_TILE),
                                _tiles(w_ffn_up[:, :, hidden:], FFN_TILE)], axis=3)
    w_pg = _tiles(w_pg, FFN_TILE)

    xf = x.reshape(n, d)
    pf = p.reshape(depth * n, -1)
    for l in range(depth):
        proj = _proj(xf, w_in, b_in, l)
        o_fw = _hgrn(proj, lower[l, 0], batch, reverse=False, col_f=col_f)
        oa = _hgrn(proj, lower[l, 1], batch, reverse=True, col_f=col_f + 1,
                   norm_g=a_norm_g[l], o_prev=o_fw)
        ob = _na(proj, _na_table(rpb[l]), batch, col_q)
        x1, x1b = _mix(oa, ob, proj, xf, w_branch, w_out, l, ln1_g[l], ln1_b[l], col_gate, alpha)
        xf = _ffn(x1, x1b, pf, l, w_ffn_up, w_ffn_down, w_pg, w_pe, ln2_g[l], ln2_b[l], alpha)
    return xf.reshape(batch, seq, d)
```

```python
import functools

import jax
import jax.numpy as jnp
from jax import lax
from jax.experimental import pallas as pl
from jax.experimental.pallas import tpu as pltpu

F32 = jnp.float32
BF16 = jnp.bfloat16

HEAD_DIM = 128
GRID_W = 64
WIN_R = 8
WIN_C = 16
ROW_BLOCK = 8
F_MIN = 1e-6
NA_LEAD = 1
LEAD = 4
LN_EPS = 1e-5
RMS_EPS = 1e-6
MASK_NEG = -1e30

VMEM_LIMIT = 60 * 1024 * 1024

_NT = (((1,), (1,)), ((), ()))


def _silu(x):
    return x * jax.nn.sigmoid(x)


def _layer_norm(y, g, b):
    mu = jnp.mean(y, axis=-1, keepdims=True)
    yc = y - mu
    var = jnp.mean(yc * yc, axis=-1, keepdims=True)
    return yc * lax.rsqrt(var + LN_EPS) * g + b


def _params(*sem):
    return pltpu.CompilerParams(dimension_semantics=sem, vmem_limit_bytes=VMEM_LIMIT)


def _proj_kernel(x_ref, w_ref, b_ref, o_ref, xb_ref):
    @pl.when(pl.program_id(1) == 0)
    def _():
        xb_ref[...] = x_ref[...].astype(BF16)

    o_ref[...] = jnp.dot(xb_ref[...], w_ref[...], preferred_element_type=F32) + b_ref[...]


def _proj(x, w, b, layer, tm=1024, tn=2048):
    n, d = x.shape
    n_in = w.shape[2]
    return pl.pallas_call(
        _proj_kernel,
        grid=(n // tm, n_in // tn),
        in_specs=[
            pl.BlockSpec((tm, d), lambda i, j: (i, 0)),
            pl.BlockSpec((None, d, tn), lambda i, j: (layer, 0, j)),
            pl.BlockSpec((None, 1, tn), lambda i, j: (layer, 0, j)),
        ],
        out_specs=pl.BlockSpec((tm, tn), lambda i, j: (i, j)),
        out_shape=jax.ShapeDtypeStruct((n, n_in), F32),
        scratch_shapes=[pltpu.VMEM((tm, d), BF16)],
        compiler_params=_params("arbitrary", "arbitrary"),
        name="proj",
    )(x, w, b)


def _hgrn_kernel(*refs, reverse, final, chunk, heads):
    if final:
        q_ref, v_ref, f_ref, lb_ref, ga_ref, ng_ref, op_ref, o_ref, st_ref, g_scr = refs
    else:
        q_ref, v_ref, f_ref, lb_ref, o_ref, st_ref, g_scr = refs
    c_len = chunk
    batch = q_ref.shape[0]

    @pl.when(pl.program_id(0) == 0)
    def _():
        st_ref[...] = jnp.zeros_like(st_ref)

    row = lax.broadcasted_iota(jnp.int32, (c_len, 1), 0)
    t_idx = lax.broadcasted_iota(jnp.int32, (c_len, c_len), 0)
    s_idx = lax.broadcasted_iota(jnp.int32, (c_len, c_len), 1)
    causal = (s_idx >= t_idx) if reverse else (s_idx <= t_idx)
    txs = t_idx ^ s_idx
    mask_diag = jnp.logical_and((txs >> 3) == 0, causal)
    pair_sizes = [c_len >> i for i in range(1, c_len.bit_length() - 3)]
    pair_masks = [jnp.logical_and((txs >> (c.bit_length() - 1)) == 1, causal) for c in pair_sizes]
    is_query = [((row & c) == 0) if reverse else ((row & c) != 0) for c in pair_sizes]

    def scores(h, b):
        hs = slice(h * HEAD_DIM, (h + 1) * HEAD_DIM)
        lb = lb_ref[:, hs]
        f = lb + (1.0 - lb) * jax.nn.sigmoid(f_ref[b, :, hs])
        f = jnp.clip(f, F_MIN, 1.0)
        kk = 1.0 - f

        g = jnp.log2(f)
        k = 1
        while k < c_len:
            if reverse:
                g = g + jnp.where(row < c_len - k, pltpu.roll(g, c_len - k, axis=0), 0.0)
            else:
                g = g + jnp.where(row >= k, pltpu.roll(g, k, axis=0), 0.0)
            k *= 2
        g_scr[b, :, hs] = g

        def ref_rows(block, offset):
            parts = [
                jnp.broadcast_to(g_scr[b, pl.ds(k * block + offset, 1), hs], (block, HEAD_DIM))
                for k in range(c_len // block)
            ]
            return parts[0] if len(parts) == 1 else jnp.concatenate(parts, axis=0)

        qh = _silu(q_ref[b, :, hs])
        gref = ref_rows(8, 4 if reverse else 3)
        xq = (qh * jnp.exp2(g - gref)).astype(BF16)
        xk = (kk * jnp.exp2(gref - g)).astype(BF16)
        att = jnp.where(mask_diag,
                        lax.dot_general(xq, xk, _NT, preferred_element_type=F32), 0.0)
        for c, m, isq in zip(pair_sizes, pair_masks, is_query):
            gref = ref_rows(2 * c, c if reverse else c - 1)
            xl = (jnp.where(isq, qh, kk) * jnp.exp2(-jnp.abs(g - gref))).astype(BF16)
            att = jnp.where(m, lax.dot_general(xl, xl, _NT, preferred_element_type=F32), att)

        g_last = g_scr[b, pl.ds(0 if reverse else c_len - 1, 1), hs]
        q_in = (qh * jnp.exp2(g)).astype(BF16)
        k_dec = (kk * jnp.exp2(g_last - g)).astype(BF16)
        return att.astype(BF16), q_in, k_dec, jnp.exp2(g_last)

    def outputs(h, b, att, q_in, k_dec, d_last):
        hs = slice(h * HEAD_DIM, (h + 1) * HEAD_DIM)
        st = st_ref[b * heads + h]
        vt = v_ref[b, :, hs].astype(BF16).T
        o = lax.dot_general(jnp.concatenate([q_in, att], axis=1),
                            jnp.concatenate([st.astype(BF16), vt], axis=1),
                            _NT, preferred_element_type=F32)
        st_ref[b * heads + h] = st * d_last + jnp.dot(vt, k_dec, preferred_element_type=F32)
        if final:
            o = o + op_ref[b, :, hs]
            ms = jnp.mean(o * o, axis=-1, keepdims=True)
            gate = ng_ref[:, hs] * _silu(ga_ref[b, :, hs])
            o_ref[b, :, hs] = (o * lax.rsqrt(ms + RMS_EPS) * gate).astype(o_ref.dtype)
        else:
            o_ref[b, :, hs] = o

    pending = []
    for h in range(heads):
        for b in range(batch):
            pending.append((h, b) + scores(h, b))
            if len(pending) > LEAD:
                outputs(*pending.pop(0))
    for item in pending:
        outputs(*item)


def _hgrn(proj, lb, batch, *, reverse, col_f, norm_g=None, o_prev=None, chunk=128):
    n, n_in = proj.shape
    seq = n // batch
    width = lb.shape[-1]
    heads = width // HEAD_DIM
    nc = seq // chunk
    final = o_prev is not None
    proj3 = proj.reshape(batch, seq, n_in)

    def col(j):
        return pl.BlockSpec((batch, chunk, width),
                            lambda i: (0, (nc - 1 - i) if reverse else i, j))

    vec = pl.BlockSpec((1, width), lambda i: (0, 0))
    in_specs = [col(0), col(1), col(col_f), vec]
    args = [proj3, proj3, proj3, lb.reshape(1, width)]
    if final:
        in_specs += [col(2), vec, col(0)]
        args += [proj3, norm_g.reshape(1, width), o_prev.reshape(batch, seq, width)]
    out = pl.pallas_call(
        functools.partial(_hgrn_kernel, reverse=reverse, final=final, chunk=chunk, heads=heads),
        grid=(nc,),
        in_specs=in_specs,
        out_specs=col(0),
        out_shape=jax.ShapeDtypeStruct((batch, seq, width), BF16 if final else F32),
        scratch_shapes=[
            pltpu.VMEM((batch * heads, HEAD_DIM, HEAD_DIM), F32),
            pltpu.VMEM((batch, chunk, width), F32),
        ],
        compiler_params=_params("arbitrary"),
        name="hgrn_bwd" if reverse else "hgrn_fwd",
    )(*args)
    return out.reshape(n, width)


HALF_ROWS = ROW_BLOCK // 2
KEY_ROWS = HALF_ROWS + WIN_R


def _na_table(rpb):
    hp = lax.Precision.HIGHEST
    nh = rpb.shape[0]
    qr4 = jnp.arange(HALF_ROWS)
    kr12 = jnp.arange(KEY_ROWS)
    col = jnp.arange(GRID_W)
    dr = kr12[None, :] - qr4[:, None] + (WIN_R - 1 - HALF_ROWS)
    dc = col[None, :] - col[:, None] + (WIN_C - 1)
    oh_r = (dr[..., None] == jnp.arange(2 * WIN_R - 1)).astype(F32)
    oh_c = (dc[..., None] == jnp.arange(2 * WIN_C - 1)).astype(F32)
    t1 = jnp.einsum('hab,rka->hrkb', rpb.astype(F32), oh_r, precision=hp)
    bias = jnp.einsum('hrkb,cjb->hrckj', t1, oh_c, precision=hp)

    a = jnp.arange(2)[:, None, None, None, None]
    qr = a * HALF_ROWS + qr4[None, :, None, None, None]
    kr = a * HALF_ROWS + kr12[None, None, None, :, None] - WIN_R // 2
    qc = col[None, None, :, None, None]
    kc = col[None, None, None, None, :]
    cs = jnp.clip(qc - WIN_C // 2, 0, GRID_W - WIN_C)
    col_ok = (kc >= cs) & (kc < cs + WIN_C)
    rs_mid = qr - WIN_R // 2
    tabs = []
    for rs in (jnp.maximum(rs_mid, 0), rs_mid, jnp.minimum(rs_mid, 0)):
        ok = (kr >= rs) & (kr < rs + WIN_R) & col_ok
        tabs.append(jnp.where(ok[None], bias[:, None], MASK_NEG))
    return jnp.stack(tabs).reshape(3, nh, ROW_BLOCK * GRID_W, KEY_ROWS * GRID_W)


def _na_kernel(q_ref, kp_ref, kc_ref, kn_ref, vp_ref, vc_ref, vn_ref, tab_ref, o_ref, *, scale):
    half = q_ref.shape[0] // 2
    nk = tab_ref.shape[2]

    def logits(h, a, q, k):
        qs = slice(a * half, (a + 1) * half)
        return lax.dot_general(q[qs], k[a * half:a * half + nk], _NT,
                               preferred_element_type=F32) + tab_ref[h, qs, :]

    def attend(h, a, s, v):
        hs = slice(h * HEAD_DIM, (h + 1) * HEAD_DIM)
        m = jnp.max(s, axis=-1, keepdims=True)
        p = jnp.exp(s - m)
        l = jnp.sum(p, axis=-1, keepdims=True)
        o = jnp.dot(p.astype(BF16), v[a * half:a * half + nk], preferred_element_type=F32) / l
        o_ref[a * half:(a + 1) * half, hs] = o.astype(o_ref.dtype)

    pending = []
    for h in range(tab_ref.shape[0]):
        hs = slice(h * HEAD_DIM, (h + 1) * HEAD_DIM)

        def window(prev, cur, nxt):
            return jnp.concatenate([prev[half:, hs], cur[:, hs], nxt[:half, hs]],
                                   axis=0).astype(BF16)

        q = (q_ref[:, hs] * scale).astype(BF16)
        k = window(kp_ref, kc_ref, kn_ref)
        v = window(vp_ref, vc_ref, vn_ref)
        for a in range(2):
            pending.append((h, a, logits(h, a, q, k), v))
            if len(pending) > NA_LEAD:
                attend(*pending.pop(0))
    for item in pending:
        attend(*item)


def _na(proj, table, batch, col_q, heads_per_step=4):
    n = proj.shape[0]
    heads = table.shape[1]
    hp = heads_per_step
    nq = ROW_BLOCK * GRID_W
    nblk = n // batch // nq
    assert nblk >= 2 and heads % hp == 0
    cq = col_q // (hp * HEAD_DIM)
    hb = heads // hp

    def spec(which, shift):
        def index(h, b, i):
            return (b * nblk + jnp.clip(i + shift, 0, nblk - 1), cq + which * hb + h)
        return pl.BlockSpec((nq, hp * HEAD_DIM), index)

    def tab_index(h, b, i):
        return (jnp.where(i == 0, 0, jnp.where(i == nblk - 1, 2, 1)), h, 0, 0)

    return pl.pallas_call(
        functools.partial(_na_kernel, scale=HEAD_DIM ** -0.5),
        grid=(hb, batch, nblk),
        in_specs=[spec(0, 0), spec(1, -1), spec(1, 0), spec(1, 1),
                  spec(2, -1), spec(2, 0), spec(2, 1),
                  pl.BlockSpec((None, hp, nq, table.shape[3]), tab_index)],
        out_specs=pl.BlockSpec((nq, hp * HEAD_DIM), lambda h, b, i: (b * nblk + i, h)),
        out_shape=jax.ShapeDtypeStruct((n, heads * HEAD_DIM), BF16),
        compiler_params=_params("arbitrary", "arbitrary", "arbitrary"),
        name="natten",
    )(proj, proj, proj, proj, proj, proj, proj, table)


def _mix_kernel(oa_ref, ob_ref, ga_ref, gb_ref, x_ref, wb_ref, wo_ref, ln_ref, o_ref, ob16_ref,
                *, alpha):
    ya = jnp.dot(oa_ref[...], wb_ref[0], preferred_element_type=F32)
    yb = jnp.dot(ob_ref[...], wb_ref[1], preferred_element_type=F32)
    merged = jax.nn.sigmoid(ga_ref[...]) * ya + jax.nn.sigmoid(gb_ref[...]) * yb
    mix = jnp.dot(merged.astype(BF16), wo_ref[...], preferred_element_type=F32)
    y = _layer_norm(alpha * x_ref[...] + mix, ln_ref[0:1, :], ln_ref[1:2, :])
    o_ref[...] = y
    ob16_ref[...] = y.astype(BF16)


def _mix(oa, ob, proj, x, w_branch, w_out, layer, ln_g, ln_b, col_gate, alpha, tm=256):
    n, d = x.shape
    wa = oa.shape[1]
    cg = col_gate // d
    const = pl.Buffered(1)
    row = pl.BlockSpec((tm, d), lambda i: (i, 0))
    return pl.pallas_call(
        functools.partial(_mix_kernel, alpha=alpha),
        grid=(n // tm,),
        in_specs=[
            pl.BlockSpec((tm, wa), lambda i: (i, 0)),
            pl.BlockSpec((tm, wa), lambda i: (i, 0)),
            pl.BlockSpec((tm, d), lambda i: (i, cg)),
            pl.BlockSpec((tm, d), lambda i: (i, cg + 1)),
            row,
            pl.BlockSpec((None, 2, wa, d), lambda i: (layer, 0, 0, 0), pipeline_mode=const),
            pl.BlockSpec((None, d, d), lambda i: (layer, 0, 0), pipeline_mode=const),
            pl.BlockSpec((2, d), lambda i: (0, 0)),
        ],
        out_specs=[row, row],
        out_shape=[jax.ShapeDtypeStruct((n, d), F32), jax.ShapeDtypeStruct((n, d), BF16)],
        compiler_params=_params("arbitrary"),
        name="mix",
    )(oa, ob, proj, proj, x, w_branch, w_out, jnp.stack([ln_g, ln_b]))


def _ffn_kernel(x_ref, xb_ref, p_ref, wg_ref, wv_ref, wd_ref, wpg_ref, wpe_ref, ln_ref, o_ref,
                *, alpha, nh):
    j = pl.program_id(1)
    tc = wpg_ref.shape[1]

    @pl.when(j == 0)
    def _():
        o_ref[...] = jnp.zeros_like(o_ref)

    @pl.when(j < nh)
    def _():
        xb = xb_ref[...]
        ug = jnp.dot(xb, wg_ref[...], preferred_element_type=F32)
        uv = jnp.dot(xb, wv_ref[...], preferred_element_type=F32)
        hid = (_silu(ug) * uv).astype(BF16)
        o_ref[...] += jnp.dot(hid, wd_ref[...], preferred_element_type=F32)

    @pl.when(j >= nh)
    def _():
        cols = pl.ds(pl.multiple_of((j - nh) * tc, tc), tc)
        pg = jnp.dot(xb_ref[...], wpg_ref[...], preferred_element_type=F32)
        pe = jnp.dot(p_ref[...].astype(BF16), wpe_ref[...], preferred_element_type=F32)
        o_ref[:, cols] += jax.nn.sigmoid(pg) * pe

    @pl.when(j == pl.num_programs(1) - 1)
    def _():
        o_ref[...] = _layer_norm(alpha * x_ref[...] + o_ref[...], ln_ref[0:1, :], ln_ref[1:2, :])


def _ffn(x, xb, p, layer, w_up, w_down, w_pg, w_pe, ln_g, ln_b, alpha, tm=1024, th=512):
    n, d = x.shape
    hidden = w_down.shape[1]
    nh = hidden // th
    nc = d // th
    p_row0 = layer * (n // tm)

    def hid(j):
        return jnp.minimum(j, nh - 1)

    def emb(j):
        return jnp.maximum(j - nh, 0)

    return pl.pallas_call(
        functools.partial(_ffn_kernel, alpha=alpha, nh=nh),
        grid=(n // tm, nh + nc),
        in_specs=[
            pl.BlockSpec((tm, d), lambda i, j: (i, 0), pipeline_mode=pl.Buffered(1)),
            pl.BlockSpec((tm, d), lambda i, j: (i, 0)),
            pl.BlockSpec((tm, p.shape[1]), lambda i, j: (p_row0 + i, 0)),
            pl.BlockSpec((None, d, th), lambda i, j: (layer, 0, hid(j))),
            pl.BlockSpec((None, d, th), lambda i, j: (layer, 0, nh + hid(j))),
            pl.BlockSpec((None, th, d), lambda i, j: (layer, hid(j), 0)),
            pl.BlockSpec((None, d, th), lambda i, j: (layer, 0, emb(j))),
            pl.BlockSpec((None, w_pe.shape[1], th), lambda i, j: (layer, 0, emb(j))),
            pl.BlockSpec((2, d), lambda i, j: (0, 0)),
        ],
        out_specs=pl.BlockSpec((tm, d), lambda i, j: (i, 0)),
        out_shape=jax.ShapeDtypeStruct((n, d), F32),
        compiler_params=_params("arbitrary", "arbitrary"),
        name="ffn",
    )(x, xb, p, w_up, w_up, w_down, w_pg, w_pe, jnp.stack([ln_g, ln_b]))


def kernel(x, p, w_in, b_in, lb_logits, a_norm_g, rpb, w_branch, w_out, ln1_g, ln1_b,
           w_ffn_up, w_ffn_down, w_pe, w_pg, ln2_g, ln2_b):
    batch, seq, d = x.shape
    depth = w_in.shape[0]
    n = batch * seq
    a_width = lb_logits.shape[-1]
    na_width = rpb.shape[1] * HEAD_DIM
    alpha = (2 * depth) ** 0.25
    col_f = 3
    col_q = 5 * a_width
    col_gate = 5 * a_width + 3 * na_width
    assert col_gate % d == 0 and (seq // GRID_W) % ROW_BLOCK == 0

    lb_sm = jax.nn.softmax(lb_logits.astype(F32), axis=0)
    lower = jnp.cumsum(lb_sm, axis=0) - lb_sm[0:1]

    w_in, w_branch, w_out, w_ffn_up, w_ffn_down, w_pg, w_pe = (
        w.astype(BF16) for w in (w_in, w_branch, w_out, w_ffn_up, w_ffn_down, w_pg, w_pe))
    b_in = b_in.reshape(depth, 1, -1)

    xf = x.reshape(n, d)
    pf = p.reshape(depth * n, -1)
    for l in range(depth):
        proj = _proj(xf, w_in, b_in, l)
        o_fw = _hgrn(proj, lower[l, 0], batch, reverse=False, col_f=col_f)
        oa = _hgrn(proj, lower[l, 1], batch, reverse=True, col_f=col_f + 1,
                   norm_g=a_norm_g[l], o_prev=o_fw)
        ob = _na(proj, _na_table(rpb[l]), batch, col_q)
        x1, x1b = _mix(oa, ob, proj, xf, w_branch, w_out, l, ln1_g[l], ln1_b[l], col_gate, alpha)
        xf = _ffn(x1, x1b, pf, l, w_ffn_up, w_ffn_down, w_pg, w_pe, ln2_g[l], ln2_b[l], alpha)
    return xf.reshape(batch, seq, d)
```

```python
import functools

import jax
import jax.numpy as jnp
from jax import lax
from jax.experimental import pallas as pl
from jax.experimental.pallas import tpu as pltpu

F32 = jnp.float32
BF16 = jnp.bfloat16

HEAD_DIM = 128
GRID_W = 64
WIN_R = 8
WIN_C = 16
ROW_BLOCK = 8
F_MIN = 1e-6
NA_LEAD = 1
LEAD = 4
LN_EPS = 1e-5
RMS_EPS = 1e-6
MASK_NEG = -1e30

VMEM_LIMIT = 60 * 1024 * 1024

_NT = (((1,), (1,)), ((), ()))


def _silu(x):
    return x * jax.nn.sigmoid(x)


def _layer_norm(y, g, b):
    mu = jnp.mean(y, axis=-1, keepdims=True)
    yc = y - mu
    var = jnp.mean(yc * yc, axis=-1, keepdims=True)
    return yc * lax.rsqrt(var + LN_EPS) * g + b


def _params(*sem):
    return pltpu.CompilerParams(dimension_semantics=sem, vmem_limit_bytes=VMEM_LIMIT)


def _proj_kernel(x_ref, w_ref, b_ref, o_ref, xb_ref):
    @pl.when(pl.program_id(1) == 0)
    def _():
        xb_ref[...] = x_ref[...].astype(BF16)

    o_ref[...] = jnp.dot(xb_ref[...], w_ref[...], preferred_element_type=F32) + b_ref[...]


def _proj(x, w, b, layer, tm=1024, tn=2048):
    n, d = x.shape
    n_in = w.shape[2]
    return pl.pallas_call(
        _proj_kernel,
        grid=(n // tm, n_in // tn),
        in_specs=[
            pl.BlockSpec((tm, d), lambda i, j: (i, 0)),
            pl.BlockSpec((None, d, tn), lambda i, j: (layer, 0, j)),
            pl.BlockSpec((None, 1, tn), lambda i, j: (layer, 0, j)),
        ],
        out_specs=pl.BlockSpec((tm, tn), lambda i, j: (i, j)),
        out_shape=jax.ShapeDtypeStruct((n, n_in), F32),
        scratch_shapes=[pltpu.VMEM((tm, d), BF16)],
        compiler_params=_params("arbitrary", "arbitrary"),
        name="proj",
    )(x, w, b)


def _hgrn_kernel(*refs, reverse, final, chunk, heads):
    if final:
        q_ref, v_ref, f_ref, lb_ref, ga_ref, ng_ref, op_ref, o_ref, st_ref, g_scr = refs
    else:
        q_ref, v_ref, f_ref, lb_ref, o_ref, st_ref, g_scr = refs
    c_len = chunk
    batch = q_ref.shape[0]

    @pl.when(pl.program_id(0) == 0)
    def _():
        st_ref[...] = jnp.zeros_like(st_ref)

    row = lax.broadcasted_iota(jnp.int32, (c_len, 1), 0)
    t_idx = lax.broadcasted_iota(jnp.int32, (c_len, c_len), 0)
    s_idx = lax.broadcasted_iota(jnp.int32, (c_len, c_len), 1)
    causal = (s_idx >= t_idx) if reverse else (s_idx <= t_idx)
    txs = t_idx ^ s_idx
    mask_diag = jnp.logical_and((txs >> 3) == 0, causal)
    pair_sizes = [c_len >> i for i in range(1, c_len.bit_length() - 3)]
    pair_masks = [jnp.logical_and((txs >> (c.bit_length() - 1)) == 1, causal) for c in pair_sizes]
    is_query = [((row & c) == 0) if reverse else ((row & c) != 0) for c in pair_sizes]

    def scores(h, b):
        hs = slice(h * HEAD_DIM, (h + 1) * HEAD_DIM)
        lb = lb_ref[:, hs]
        f = lb + (1.0 - lb) * jax.nn.sigmoid(f_ref[b, :, hs])
        f = jnp.clip(f, F_MIN, 1.0)
        kk = 1.0 - f

        g = jnp.log2(f)
        k = 1
        while k < c_len:
            if reverse:
                g = g + jnp.where(row < c_len - k, pltpu.roll(g, c_len - k, axis=0), 0.0)
            else:
                g = g + jnp.where(row >= k, pltpu.roll(g, k, axis=0), 0.0)
            k *= 2
        g_scr[b, :, hs] = g

        def ref_rows(block, offset):
            parts = [
                jnp.broadcast_to(g_scr[b, pl.ds(k * block + offset, 1), hs], (block, HEAD_DIM))
                for k in range(c_len // block)
            ]
            return parts[0] if len(parts) == 1 else jnp.concatenate(parts, axis=0)

        qh = _silu(q_ref[b, :, hs])
        gref = ref_rows(8, 4 if reverse else 3)
        xq = (qh * jnp.exp2(g - gref)).astype(BF16)
        xk = (kk * jnp.exp2(gref - g)).astype(BF16)
        att = jnp.where(mask_diag,
                        lax.dot_general(xq, xk, _NT, preferred_element_type=F32), 0.0)
        for c, m, isq in zip(pair_sizes, pair_masks, is_query):
            gref = ref_rows(2 * c, c if reverse else c - 1)
            xl = (jnp.where(isq, qh, kk) * jnp.exp2(-jnp.abs(g - gref))).astype(BF16)
            att = jnp.where(m, lax.dot_general(xl, xl, _NT, preferred_element_type=F32), att)

        g_last = g_scr[b, pl.ds(0 if reverse else c_len - 1, 1), hs]
        q_in = (qh * jnp.exp2(g)).astype(BF16)
        k_dec = (kk * jnp.exp2(g_last - g)).astype(BF16)
        return att.astype(BF16), q_in, k_dec, jnp.exp2(g_last)

    def outputs(h, b, att, q_in, k_dec, d_last):
        hs = slice(h * HEAD_DIM, (h + 1) * HEAD_DIM)
        st = st_ref[b * heads + h]
        vt = v_ref[b, :, hs].astype(BF16).T
        o = lax.dot_general(jnp.concatenate([q_in, att], axis=1),
                            jnp.concatenate([st.astype(BF16), vt], axis=1),
                            _NT, preferred_element_type=F32)
        st_ref[b * heads + h] = st * d_last + jnp.dot(vt, k_dec, preferred_element_type=F32)
        if final:
            o = o + op_ref[b, :, hs]
            ms = jnp.mean(o * o, axis=-1, keepdims=True)
            gate = ng_ref[:, hs] * _silu(ga_ref[b, :, hs])
            o_ref[b, :, hs] = (o * lax.rsqrt(ms + RMS_EPS) * gate).astype(o_ref.dtype)
        else:
            o_ref[b, :, hs] = o

    pending = []
    for h in range(heads):
        for b in range(batch):
            pending.append((h, b) + scores(h, b))
            if len(pending) > LEAD:
                outputs(*pending.pop(0))
    for item in pending:
        outputs(*item)


def _hgrn(proj, lb, batch, *, reverse, col_f, norm_g=None, o_prev=None, chunk=128):
    n, n_in = proj.shape
    seq = n // batch
    width = lb.shape[-1]
    heads = width // HEAD_DIM
    nc = seq // chunk
    final = o_prev is not None
    proj3 = proj.reshape(batch, seq, n_in)

    def col(j):
        return pl.BlockSpec((batch, chunk, width),
                            lambda i: (0, (nc - 1 - i) if reverse else i, j))

    vec = pl.BlockSpec((1, width), lambda i: (0, 0))
    in_specs = [col(0), col(1), col(col_f), vec]
    args = [proj3, proj3, proj3, lb.reshape(1, width)]
    if final:
        in_specs += [col(2), vec, col(0)]
        args += [proj3, norm_g.reshape(1, width), o_prev.reshape(batch, seq, width)]
    out = pl.pallas_call(
        functools.partial(_hgrn_kernel, reverse=reverse, final=final, chunk=chunk, heads=heads),
        grid=(nc,),
        in_specs=in_specs,
        out_specs=col(0),
        out_shape=jax.ShapeDtypeStruct((batch, seq, width), BF16 if final else F32),
        scratch_shapes=[
            pltpu.VMEM((batch * heads, HEAD_DIM, HEAD_DIM), F32),
            pltpu.VMEM((batch, chunk, width), F32),
        ],
        compiler_params=_params("arbitrary"),
        name="hgrn_bwd" if reverse else "hgrn_fwd",
    )(*args)
    return out.reshape(n, width)


HALF_ROWS = ROW_BLOCK // 2
KEY_ROWS = HALF_ROWS + WIN_R


def _na_table(rpb):
    hp = lax.Precision.HIGHEST
    nh = rpb.shape[0]
    qr4 = jnp.arange(HALF_ROWS)
    kr12 = jnp.arange(KEY_ROWS)
    col = jnp.arange(GRID_W)
    dr = kr12[None, :] - qr4[:, None] + (WIN_R - 1 - HALF_ROWS)
    dc = col[None, :] - col[:, None] + (WIN_C - 1)
    oh_r = (dr[..., None] == jnp.arange(2 * WIN_R - 1)).astype(F32)
    oh_c = (dc[..., None] == jnp.arange(2 * WIN_C - 1)).astype(F32)
    t1 = jnp.einsum('hab,rka->hrkb', rpb.astype(F32), oh_r, precision=hp)
    bias = jnp.einsum('hrkb,cjb->hrckj', t1, oh_c, precision=hp)

    a = jnp.arange(2)[:, None, None, None, None]
    qr = a * HALF_ROWS + qr4[None, :, None, None, None]
    kr = a * HALF_ROWS + kr12[None, None, None, :, None] - WIN_R // 2
    qc = col[None, None, :, None, None]
    kc = col[None, None, None, None, :]
    cs = jnp.clip(qc - WIN_C // 2, 0, GRID_W - WIN_C)
    col_ok = (kc >= cs) & (kc < cs + WIN_C)
    rs_mid = qr - WIN_R // 2
    tabs = []
    for rs in (jnp.maximum(rs_mid, 0), rs_mid, jnp.minimum(rs_mid, 0)):
        ok = (kr >= rs) & (kr < rs + WIN_R) & col_ok
        tabs.append(jnp.where(ok[None], bias[:, None], MASK_NEG))
    return jnp.stack(tabs).reshape(3, nh, ROW_BLOCK * GRID_W, KEY_ROWS * GRID_W)


def _na_kernel(q_ref, kp_ref, kc_ref, kn_ref, vp_ref, vc_ref, vn_ref, tab_ref, o_ref, *, scale):
    half = q_ref.shape[0] // 2
    nk = tab_ref.shape[2]

    def logits(h, a, q, k):
        qs = slice(a * half, (a + 1) * half)
        return lax.dot_general(q[qs], k[a * half:a * half + nk], _NT,
                               preferred_element_type=F32) + tab_ref[h, qs, :]

    def attend(h, a, s, v):
        hs = slice(h * HEAD_DIM, (h + 1) * HEAD_DIM)
        m = jnp.max(s, axis=-1, keepdims=True)
        p = jnp.exp(s - m)
        l = jnp.sum(p, axis=-1, keepdims=True)
        o = jnp.dot(p.astype(BF16), v[a * half:a * half + nk], preferred_element_type=F32) / l
        o_ref[a * half:(a + 1) * half, hs] = o.astype(o_ref.dtype)

    pending = []
    for h in range(tab_ref.shape[0]):
        hs = slice(h * HEAD_DIM, (h + 1) * HEAD_DIM)

        def window(prev, cur, nxt):
            return jnp.concatenate([prev[half:, hs], cur[:, hs], nxt[:half, hs]],
                                   axis=0).astype(BF16)

        q = (q_ref[:, hs] * scale).astype(BF16)
        k = window(kp_ref, kc_ref, kn_ref)
        v = window(vp_ref, vc_ref, vn_ref)
        for a in range(2):
            pending.append((h, a, logits(h, a, q, k), v))
            if len(pending) > NA_LEAD:
                attend(*pending.pop(0))
    for item in pending:
        attend(*item)


def _na(proj, table, batch, col_q, heads_per_step=4):
    n = proj.shape[0]
    heads = table.shape[1]
    hp = heads_per_step
    nq = ROW_BLOCK * GRID_W
    nblk = n // batch // nq
    assert nblk >= 2 and heads % hp == 0
    cq = col_q // (hp * HEAD_DIM)
    hb = heads // hp

    def spec(which, shift):
        def index(h, b, i):
            return (b * nblk + jnp.clip(i + shift, 0, nblk - 1), cq + which * hb + h)
        return pl.BlockSpec((nq, hp * HEAD_DIM), index)

    def tab_index(h, b, i):
        return (jnp.where(i == 0, 0, jnp.where(i == nblk - 1, 2, 1)), h, 0, 0)

    return pl.pallas_call(
        functools.partial(_na_kernel, scale=HEAD_DIM ** -0.5),
        grid=(hb, batch, nblk),
        in_specs=[spec(0, 0), spec(1, -1), spec(1, 0), spec(1, 1),
                  spec(2, -1), spec(2, 0), spec(2, 1),
                  pl.BlockSpec((None, hp, nq, table.shape[3]), tab_index)],
        out_specs=pl.BlockSpec((nq, hp * HEAD_DIM), lambda h, b, i: (b * nblk + i, h)),
        out_shape=jax.ShapeDtypeStruct((n, heads * HEAD_DIM), BF16),
        compiler_params=_params("arbitrary", "arbitrary", "arbitrary"),
        name="natten",
    )(proj, proj, proj, proj, proj, proj, proj, table)


def _mix_kernel(oa_ref, ob_ref, ga_ref, gb_ref, x_ref, wb_ref, wo_ref, ln_ref, o_ref, ob16_ref,
                *, alpha):
    ya = jnp.dot(oa_ref[...], wb_ref[0], preferred_element_type=F32)
    yb = jnp.dot(ob_ref[...], wb_ref[1], preferred_element_type=F32)
    merged = jax.nn.sigmoid(ga_ref[...]) * ya + jax.nn.sigmoid(gb_ref[...]) * yb
    mix = jnp.dot(merged.astype(BF16), wo_ref[...], preferred_element_type=F32)
    y = _layer_norm(alpha * x_ref[...] + mix, ln_ref[0:1, :], ln_ref[1:2, :])
    o_ref[...] = y
    ob16_ref[...] = y.astype(BF16)


def _mix(oa, ob, proj, x, w_branch, w_out, layer, ln_g, ln_b, col_gate, alpha, tm=256):
    n, d = x.shape
    wa = oa.shape[1]
    cg = col_gate // d
    const = pl.Buffered(1)
    row = pl.BlockSpec((tm, d), lambda i: (i, 0))
    return pl.pallas_call(
        functools.partial(_mix_kernel, alpha=alpha),
        grid=(n // tm,),
        in_specs=[
            pl.BlockSpec((tm, wa), lambda i: (i, 0)),
            pl.BlockSpec((tm, wa), lambda i: (i, 0)),
            pl.BlockSpec((tm, d), lambda i: (i, cg)),
            pl.BlockSpec((tm, d), lambda i: (i, cg + 1)),
            row,
            pl.BlockSpec((None, 2, wa, d), lambda i: (layer, 0, 0, 0), pipeline_mode=const),
            pl.BlockSpec((None, d, d), lambda i: (layer, 0, 0), pipeline_mode=const),
            pl.BlockSpec((2, d), lambda i: (0, 0)),
        ],
        out_specs=[row, row],
        out_shape=[jax.ShapeDtypeStruct((n, d), F32), jax.ShapeDtypeStruct((n, d), BF16)],
        compiler_params=_params("arbitrary"),
        name="mix",
    )(oa, ob, proj, proj, x, w_branch, w_out, jnp.stack([ln_g, ln_b]))


def _ffn_kernel(x_ref, xb_ref, p_ref, wg_ref, wv_ref, wd_ref, wpg_ref, wpe_ref, ln_ref, o_ref,
                *, alpha, nh):
    j = pl.program_id(1)
    tc = wpg_ref.shape[1]

    @pl.when(j == 0)
    def _():
        o_ref[...] = jnp.zeros_like(o_ref)

    @pl.when(j < nh)
    def _():
        xb = xb_ref[...]
        half = wd_ref.shape[0] // 2
        for c in range(2):
            cols = slice(c * half, (c + 1) * half)
            ug = jnp.dot(xb, wg_ref[:, cols], preferred_element_type=F32)
            uv = jnp.dot(xb, wv_ref[:, cols], preferred_element_type=F32)
            hid = (_silu(ug) * uv).astype(BF16)
            o_ref[...] += jnp.dot(hid, wd_ref[cols, :], preferred_element_type=F32)

    @pl.when(j >= nh)
    def _():
        cols = pl.ds(pl.multiple_of((j - nh) * tc, tc), tc)
        pg = jnp.dot(xb_ref[...], wpg_ref[...], preferred_element_type=F32)
        pe = jnp.dot(p_ref[...].astype(BF16), wpe_ref[...], preferred_element_type=F32)
        o_ref[:, cols] += jax.nn.sigmoid(pg) * pe

    @pl.when(j == pl.num_programs(1) - 1)
    def _():
        o_ref[...] = _layer_norm(alpha * x_ref[...] + o_ref[...], ln_ref[0:1, :], ln_ref[1:2, :])


def _ffn(x, xb, p, layer, w_up, w_down, w_pg, w_pe, ln_g, ln_b, alpha, tm=1024, th=512):
    n, d = x.shape
    hidden = w_down.shape[1]
    nh = hidden // th
    nc = d // th
    p_row0 = layer * (n // tm)

    def hid(j):
        return jnp.minimum(j, nh - 1)

    def emb(j):
        return jnp.maximum(j - nh, 0)

    return pl.pallas_call(
        functools.partial(_ffn_kernel, alpha=alpha, nh=nh),
        grid=(n // tm, nh + nc),
        in_specs=[
            pl.BlockSpec((tm, d), lambda i, j: (i, 0), pipeline_mode=pl.Buffered(1)),
            pl.BlockSpec((tm, d), lambda i, j: (i, 0)),
            pl.BlockSpec((tm, p.shape[1]), lambda i, j: (p_row0 + i, 0)),
            pl.BlockSpec((None, d, th), lambda i, j: (layer, 0, hid(j))),
            pl.BlockSpec((None, d, th), lambda i, j: (layer, 0, nh + hid(j))),
            pl.BlockSpec((None, th, d), lambda i, j: (layer, hid(j), 0)),
            pl.BlockSpec((None, d, th), lambda i, j: (layer, 0, emb(j))),
            pl.BlockSpec((None, w_pe.shape[1], th), lambda i, j: (layer, 0, emb(j))),
            pl.BlockSpec((2, d), lambda i, j: (0, 0)),
        ],
        out_specs=pl.BlockSpec((tm, d), lambda i, j: (i, 0)),
        out_shape=jax.ShapeDtypeStruct((n, d), F32),
        compiler_params=_params("arbitrary", "arbitrary"),
        name="ffn",
    )(x, xb, p, w_up, w_up, w_down, w_pg, w_pe, jnp.stack([ln_g, ln_b]))


def kernel(x, p, w_in, b_in, lb_logits, a_norm_g, rpb, w_branch, w_out, ln1_g, ln1_b,
           w_ffn_up, w_ffn_down, w_pe, w_pg, ln2_g, ln2_b):
    batch, seq, d = x.shape
    depth = w_in.shape[0]
    n = batch * seq
    a_width = lb_logits.shape[-1]
    na_width = rpb.shape[1] * HEAD_DIM
    alpha = (2 * depth) ** 0.25
    col_f = 3
    col_q = 5 * a_width
    col_gate = 5 * a_width + 3 * na_width
    assert col_gate % d == 0 and (seq // GRID_W) % ROW_BLOCK == 0

    lb_sm = jax.nn.softmax(lb_logits.astype(F32), axis=0)
    lower = jnp.cumsum(lb_sm, axis=0) - lb_sm[0:1]

    w_in, w_branch, w_out, w_ffn_up, w_ffn_down, w_pg, w_pe = (
        w.astype(BF16) for w in (w_in, w_branch, w_out, w_ffn_up, w_ffn_down, w_pg, w_pe))
    b_in = b_in.reshape(depth, 1, -1)

    xf = x.reshape(n, d)
    pf = p.reshape(depth * n, -1)
    for l in range(depth):
        proj = _proj(xf, w_in, b_in, l)
        o_fw = _hgrn(proj, lower[l, 0], batch, reverse=False, col_f=col_f)
        oa = _hgrn(proj, lower[l, 1], batch, reverse=True, col_f=col_f + 1,
                   norm_g=a_norm_g[l], o_prev=o_fw)
        ob = _na(proj, _na_table(rpb[l]), batch, col_q)
        x1, x1b = _mix(oa, ob, proj, xf, w_branch, w_out, l, ln1_g[l], ln1_b[l], col_gate, alpha)
        xf = _ffn(x1, x1b, pf, l, w_ffn_up, w_ffn_down, w_pg, w_pe, ln2_g[l], ln2_b[l], alpha)
    return xf.reshape(batch, seq, d)
```
